```python
import math
import jax
import jax.numpy as jnp
from jax import lax
import numpy as np

D_MODEL = 1024
BATCH = 2
SEQ = 8192
DEPTH = 1

CHUNK = 64
EPS = 1e-6

A_HEADS = 8
A_DK = 128
A_DV = 128
A_QK = A_HEADS * A_DK
A_V = A_HEADS * A_DV
CONV_W = 4
CONV_CH = 2 * A_QK + A_V

B_HEADS = 8
B_DK = 128
B_DV = 128
B_K = B_HEADS * B_DK
B_V = B_HEADS * B_DV

IN_SPLITS = (A_QK, A_QK, A_V, A_HEADS, A_HEADS, A_V, B_K, B_V, B_K, B_V, D_MODEL, D_MODEL)
IN_WIDTH = 3 * A_QK + 2 * A_V + 2 * A_HEADS + 2 * B_K + 2 * B_V + 2 * D_MODEL

N_KEYS = 128
N_EXPERTS = N_KEYS * N_KEYS
P_HEADS = 8
P_DKEY = 256
P_DHALF = P_DKEY // 2
P_TOPK = 16
P_TOKEN_BLOCK = 128

kernel_name = 'hybrid_gdn_hgrn2_peer_block'


def _rmsnorm(x, g):
    xf = x.astype(jnp.float32)
    y = xf * lax.rsqrt(jnp.mean(xf * xf, axis=-1, keepdims=True) + EPS) * g.astype(jnp.float32)
    return y.astype(x.dtype)


def _l2norm(x):
    return x * lax.rsqrt(jnp.sum(x * x, axis=-1, keepdims=True) + EPS)


def _split_cols(p):
    outs = []
    off = 0
    for w in IN_SPLITS:
        outs.append(p[..., off:off + w])
        off += w
    return outs


def _causal_conv_silu(x, w):
    k_w = w.shape[0]
    t = x.shape[1]
    xp = jnp.pad(x, ((0, 0), (k_w - 1, 0), (0, 0)))
    y = sum(xp[:, j:j + t] * w[j] for j in range(k_w))
    return jax.nn.silu(y)


def _to_chunks(x):
    b, t, h, d = x.shape
    return x.reshape(b, t // CHUNK, CHUNK, h, d).transpose(1, 0, 3, 2, 4)


def _to_chunks_s(x):
    b, t, h = x.shape
    return x.reshape(b, t // CHUNK, CHUNK, h).transpose(1, 0, 3, 2)


def _from_chunks(o):
    n, b, h, c, d = o.shape
    return o.transpose(1, 0, 3, 2, 4).reshape(b, n * c, h, d)


def _gated_delta_rule(q, k, v, beta, g):
    dk = q.shape[-1]
    dv = v.shape[-1]
    q = _l2norm(q) * (dk ** -0.5)
    k = _l2norm(k)
    qc, kc, vc = _to_chunks(q), _to_chunks(k), _to_chunks(v)
    bc = _to_chunks_s(beta)
    G = jnp.cumsum(_to_chunks_s(g), axis=-1)
    causal = jnp.tril(jnp.ones((CHUNK, CHUNK), dtype=bool))
    diff = G[..., :, None] - G[..., None, :]
    decay = jnp.where(causal, jnp.exp(jnp.where(causal, diff, 0.0)), 0.0)
    kb = kc * bc[..., None]
    a_kk = jnp.einsum('nbhid,nbhjd->nbhij', kb, kc) * decay
    lhs = jnp.eye(CHUNK, dtype=jnp.float32) + jnp.tril(a_kk, -1)
    rhs = jnp.concatenate([vc * bc[..., None], kb * jnp.exp(G)[..., None]], axis=-1)
    sol = lax.linalg.triangular_solve(lhs, rhs, left_side=True, lower=True, unit_diagonal=True)
    u, w = sol[..., :dv], sol[..., dv:]
    a_qk = jnp.einsum('nbhid,nbhjd->nbhij', qc, kc) * decay
    q_dec = qc * jnp.exp(G)[..., None]
    k_dec = kc * jnp.exp(G[..., -1:] - G)[..., None]
    g_last = jnp.exp(G[..., -1])

    def step(S, inp):
        u_n, w_n, aqk_n, qd_n, kd_n, gl_n = inp
        v_new = u_n - jnp.einsum('bhcd,bhde->bhce', w_n, S)
        o = jnp.einsum('bhcd,bhde->bhce', qd_n, S) + jnp.einsum('bhij,bhje->bhie', aqk_n, v_new)
        S = S * gl_n[..., None, None] + jnp.einsum('bhcd,bhce->bhde', kd_n, v_new)
        return S, o

    s0 = jnp.zeros((q.shape[0], q.shape[2], dk, dv), jnp.float32)
    _, o = lax.scan(step, s0, (u, w, a_qk, q_dec, k_dec, g_last))
    return _from_chunks(o)


def _hgrn2_recurrence(q, k, v, log_f):
    dk = q.shape[-1]
    dv = v.shape[-1]
    qc, kc, vc = _to_chunks(q), _to_chunks(k), _to_chunks(v)
    bc = jnp.cumsum(_to_chunks(log_f), axis=-2)
    causal3 = jnp.tril(jnp.ones((CHUNK, CHUNK), dtype=bool))[:, :, None]

    def step(S, inp):
        q_n, k_n, v_n, b_n = inp
        diff = b_n[..., :, None, :] - b_n[..., None, :, :]
        dec = jnp.where(causal3, jnp.exp(jnp.where(causal3, diff, 0.0)), 0.0)
        a = jnp.einsum('bhid,bhjd,bhijd->bhij', q_n, k_n, dec)
        o = jnp.einsum('bhid,bhde->bhie', q_n * jnp.exp(b_n), S) + jnp.einsum('bhij,bhje->bhie', a, v_n)
        b_last = b_n[..., -1:, :]
        S = S * jnp.exp(b_last)[..., 0, :, None] + jnp.einsum('bhjd,bhje->bhde', k_n * jnp.exp(b_last - b_n), v_n)
        return S, o

    s0 = jnp.zeros((q.shape[0], q.shape[2], dk, dv), jnp.float32)
    _, o = lax.scan(step, s0, (qc, kc, vc, bc))
    return _from_chunks(o)


def _peer(x, w_pq, sub_keys, expert_u, expert_v):
    b, t, d = x.shape
    n_tok = b * t
    xf = x.reshape(n_tok, d)
    q = (xf @ w_pq).reshape(n_tok, P_HEADS, 2, P_DHALF)
    s = jnp.einsum('thpd,phkd->thpk', q, sub_keys)
    top_s, top_i = lax.top_k(s, P_TOPK)
    cand_s = (top_s[:, :, 0, :, None] + top_s[:, :, 1, None, :]).reshape(n_tok, P_HEADS, P_TOPK * P_TOPK)
    cand_i = (top_i[:, :, 0, :, None] * N_KEYS + top_i[:, :, 1, None, :]).reshape(n_tok, P_HEADS, P_TOPK * P_TOPK)
    best_s, pos = lax.top_k(cand_s, P_TOPK)
    idx = jnp.take_along_axis(cand_i, pos, axis=-1)
    gate = jax.nn.softmax(best_s.astype(jnp.float32), axis=-1).astype(x.dtype)
    n_blk = n_tok // P_TOKEN_BLOCK
    xb = xf.reshape(n_blk, P_TOKEN_BLOCK, d)
    ib = idx.reshape(n_blk, P_TOKEN_BLOCK, P_HEADS * P_TOPK)
    gb = gate.reshape(n_blk, P_TOKEN_BLOCK, P_HEADS * P_TOPK)

    def block(args):
        xt, it, gt = args
        u = expert_u[it]
        h = jax.nn.gelu(jnp.einsum('td,tkd->tk', xt, u), approximate=False)
        v = expert_v[it]
        return jnp.einsum('tk,tkd->td', gt * h, v)

    y = lax.map(block, (xb, ib, gb))
    return y.reshape(b, t, d)


def setup_inputs(seed: int = 0) -> dict:
    key = jax.random.key(seed)
    ks = jax.random.split(key, 20)

    def nrm(k, shape, scale):
        return jax.random.normal(k, shape, jnp.float32) * scale

    x = nrm(ks[0], (BATCH, SEQ, D_MODEL), 1.0)
    norm1 = 1.0 + nrm(ks[1], (DEPTH, D_MODEL), 0.02)
    w_in = nrm(ks[2], (DEPTH, D_MODEL, IN_WIDTH), D_MODEL ** -0.5)
    conv_a = nrm(ks[3], (DEPTH, CONV_W, CONV_CH), CONV_W ** -0.5)
    a_log = jnp.log(jax.random.uniform(ks[4], (DEPTH, A_HEADS), jnp.float32, 1.0, 16.0))
    dt = jnp.exp(jax.random.uniform(ks[5], (DEPTH, A_HEADS), jnp.float32, math.log(1e-3), math.log(1e-1)))
    dt_bias = dt + jnp.log(-jnp.expm1(-dt))
    a_onorm = 1.0 + nrm(ks[6], (DEPTH, A_DV), 0.02)
    b_lower_bound = 1.0 + nrm(ks[7], (DEPTH + 1, B_K), 0.1)
    b_onorm = 1.0 + nrm(ks[8], (DEPTH, B_DV), 0.02)
    w_branch_a = nrm(ks[9], (DEPTH, A_V, D_MODEL), A_V ** -0.5)
    w_branch_b = nrm(ks[10], (DEPTH, B_V, D_MODEL), B_V ** -0.5)
    w_out = nrm(ks[11], (DEPTH, D_MODEL, D_MODEL), D_MODEL ** -0.5)
    norm2 = 1.0 + nrm(ks[12], (DEPTH, D_MODEL), 0.02)
    w_pq = nrm(ks[13], (DEPTH, D_MODEL, P_HEADS * P_DKEY), D_MODEL ** -0.5)
    sub_keys = nrm(ks[14], (DEPTH, 2, P_HEADS, N_KEYS, P_DHALF), P_DHALF ** -0.5)
    expert_u = nrm(ks[15], (DEPTH, N_EXPERTS, D_MODEL), D_MODEL ** -0.5)
    expert_v = nrm(ks[16], (DEPTH, N_EXPERTS, D_MODEL), P_HEADS ** -0.5)
    final_norm = 1.0 + nrm(ks[17], (D_MODEL,), 0.02)
    return {'x': x, 'norm1': norm1, 'w_in': w_in, 'conv_a': conv_a, 'a_log': a_log,
            'dt_bias': dt_bias, 'a_onorm': a_onorm, 'b_lower_bound': b_lower_bound,
            'b_onorm': b_onorm, 'w_branch_a': w_branch_a, 'w_branch_b': w_branch_b,
            'w_out': w_out, 'norm2': norm2, 'w_pq': w_pq, 'sub_keys': sub_keys,
            'expert_u': expert_u, 'expert_v': expert_v, 'final_norm': final_norm}


def reference(x, norm1, w_in, conv_a, a_log, dt_bias, a_onorm, b_lower_bound, b_onorm,
              w_branch_a, w_branch_b, w_out, norm2, w_pq, sub_keys, expert_u, expert_v,
              final_norm):
    dt = x.dtype
    bsz, t, _ = x.shape
    f32 = jnp.float32
    lower_bounds = jnp.cumsum(jax.nn.softmax(b_lower_bound.astype(f32), axis=0), axis=0)
    for l in range(DEPTH):
        h = _rmsnorm(x, norm1[l])
        proj = h @ w_in[l]
        a_q, a_k, a_v, a_beta, a_alpha, a_gate, b_f, b_i, b_q, b_gate, g_a, g_b = _split_cols(proj)

        qkv = _causal_conv_silu(jnp.concatenate([a_q, a_k, a_v], axis=-1), conv_a[l]).astype(f32)
        qa = qkv[..., :A_QK].reshape(bsz, t, A_HEADS, A_DK)
        ka = qkv[..., A_QK:2 * A_QK].reshape(bsz, t, A_HEADS, A_DK)
        va = qkv[..., 2 * A_QK:].reshape(bsz, t, A_HEADS, A_DV)
        beta = jax.nn.sigmoid(a_beta.astype(f32))
        log_alpha = -jnp.exp(a_log[l].astype(f32)) * jax.nn.softplus(a_alpha.astype(f32) + dt_bias[l].astype(f32))
        o_a = _gated_delta_rule(qa, ka, va, beta, log_alpha)
        o_a = _rmsnorm(o_a, a_onorm[l]) * jax.nn.silu(a_gate.astype(f32)).reshape(bsz, t, A_HEADS, A_DV)
        y_a = o_a.reshape(bsz, t, A_V).astype(dt) @ w_branch_a[l]

        lb = lower_bounds[l].reshape(B_HEADS, B_DK)
        f_gate = lb + (1.0 - lb) * jax.nn.sigmoid(b_f.astype(f32).reshape(bsz, t, B_HEADS, B_DK))
        qb = jax.nn.silu(b_q.astype(f32)).reshape(bsz, t, B_HEADS, B_DK)
        ib = b_i.astype(f32).reshape(bsz, t, B_HEADS, B_DV)
        o_b = _hgrn2_recurrence(qb, 1.0 - f_gate, ib, jnp.log(f_gate))
        o_b = _rmsnorm(o_b, b_onorm[l]) * jax.nn.sigmoid(b_gate.astype(f32)).reshape(bsz, t, B_HEADS, B_DV)
        y_b = o_b.reshape(bsz, t, B_V).astype(dt) @ w_branch_b[l]

        mix = jax.nn.sigmoid(g_a) * y_a + jax.nn.sigmoid(g_b) * y_b
        x = x + mix @ w_out[l]

        h2 = _rmsnorm(x, norm2[l])
        x = x + _peer(h2, w_pq[l], sub_keys[l], expert_u[l], expert_v[l])
    return _rmsnorm(x, final_norm)
```

```python
import functools

import jax
import jax.numpy as jnp
from jax import lax
from jax.experimental import pallas as pl
from jax.experimental.pallas import tpu as pltpu

F32 = jnp.float32
BF16 = jnp.bfloat16
EPS = 1e-6

D_MODEL = 1024
HEADS = 8
HEAD_DIM = 128
CHUNK = 64
CONV_W = 4
SUB = 16

N_KEYS = 128
P_HEADS = 8
P_TOPK = 16
N_CAND = 17

VMEM_LIMIT = 56 * 1024 * 1024

_NT = (((1,), (1,)), ((), ()))
_TN = (((0,), (0,)), ((), ()))
_HI = lax.Precision.HIGHEST


def _mm(a, b):
    return jnp.dot(a.astype(BF16), b.astype(BF16), preferred_element_type=F32)


def _mm_nt(a, b):
    return lax.dot_general(a.astype(BF16), b.astype(BF16), _NT, preferred_element_type=F32)


def _mm_tn(a, b):
    return lax.dot_general(a.astype(BF16), b.astype(BF16), _TN, preferred_element_type=F32)


def _mm_hi(a, b):
    return jnp.dot(a, b, precision=_HI, preferred_element_type=F32)


def _sigmoid(x):
    return 1.0 / (1.0 + jnp.exp(-x))


def _softplus(x):
    return jnp.maximum(x, 0.0) + jnp.log(1.0 + jnp.exp(-jnp.abs(x)))


def _rms(x, g):
    return x * lax.rsqrt(jnp.mean(x * x, axis=-1, keepdims=True) + EPS) * g


def _tri(n, strict=False):
    r = lax.broadcasted_iota(jnp.int32, (n, n), 0)
    c = lax.broadcasted_iota(jnp.int32, (n, n), 1)
    return (r > c) if strict else (r >= c)


def _inproj_kernel(x_ref, g_ref, w_ref, ws_ref, wst_ref, o_ref, os_ref, ost_ref, h_scr):
    @pl.when(pl.program_id(1) == 0)
    def _():
        hb = _rms(x_ref[...], g_ref[...]).astype(BF16)
        h_scr[...] = hb
        os_ref[...] = jnp.dot(hb, ws_ref[...], preferred_element_type=F32)
        ost_ref[...] = lax.dot_general(wst_ref[...], hb, _NT, preferred_element_type=F32)

    o_ref[...] = jnp.dot(h_scr[...], w_ref[...], preferred_element_type=F32)


def _inproj(x2, g, w_main, w_small, w_small_t, tm=1024, tn=1024):
    nt = x2.shape[0]
    ncol = w_main.shape[1]
    return pl.pallas_call(
        _inproj_kernel,
        grid=(nt // tm, ncol // tn),
        in_specs=[
            pl.BlockSpec((tm, D_MODEL), lambda i, j: (i, 0)),
            pl.BlockSpec((1, D_MODEL), lambda i, j: (0, 0)),
            pl.BlockSpec((D_MODEL, tn), lambda i, j: (0, j)),
            pl.BlockSpec((D_MODEL, 128), lambda i, j: (0, 0)),
            pl.BlockSpec((16, D_MODEL), lambda i, j: (0, 0)),
        ],
        out_specs=[
            pl.BlockSpec((tm, tn), lambda i, j: (i, j)),
            pl.BlockSpec((tm, 128), lambda i, j: (i, 0)),
            pl.BlockSpec((16, tm), lambda i, j: (0, i)),
        ],
        out_shape=[
            jax.ShapeDtypeStruct((nt, ncol), F32),
            jax.ShapeDtypeStruct((nt, 128), F32),
            jax.ShapeDtypeStruct((16, nt), F32),
        ],
        scratch_shapes=[pltpu.VMEM((tm, D_MODEL), BF16)],
        compiler_params=pltpu.CompilerParams(
            dimension_semantics=("arbitrary", "arbitrary"), vmem_limit_bytes=VMEM_LIMIT),
        name="inproj",
    )(x2, g, w_main, w_small, w_small_t)


def _gdn_kernel(q_ref, k_ref, v_ref, gate_ref, sm_ref, smt_ref, conv_ref, alog_r, dtb_r,
                alog_c, dtb_c, onorm_ref, o_ref, s_scr, tail_scr, qkv_scr, grow_scr, *, tblk):
    nc = tblk // CHUNK

    @pl.when(pl.program_id(1) == 0)
    def _():
        s_scr[...] = jnp.zeros_like(s_scr)
        tail_scr[...] = jnp.zeros_like(tail_scr)

    for idx, ref in enumerate((q_ref, k_ref, v_ref)):
        cols = slice(idx * D_MODEL, (idx + 1) * D_MODEL)
        x = ref[...]
        xc = jnp.concatenate([tail_scr[:, cols], x], axis=0)
        w = conv_ref[:, cols]
        y = x * w[CONV_W - 1:CONV_W]
        for j in range(CONV_W - 1):
            off = 8 - (CONV_W - 1) + j
            y = y + xc[off:off + tblk] * w[j:j + 1]
        qkv_scr[:, cols] = y * _sigmoid(y)
        tail_scr[:, cols] = x[tblk - 8:]

    g_t = -jnp.exp(alog_c[...]) * _softplus(smt_ref[...] + dtb_c[...])
    r = lax.broadcasted_iota(jnp.int32, (tblk, tblk), 0)
    c = lax.broadcasted_iota(jnp.int32, (tblk, tblk), 1)
    same_chunk = jnp.right_shift(r, 6) == jnp.right_shift(c, 6)
    ublk = jnp.where(r <= c, jnp.where(same_chunk, 1.0, 0.0), 0.0).astype(F32)
    grow = _mm_hi(g_t, ublk)
    for ci in range(nc):
        grow_scr[ci] = grow[:, ci * CHUNK:(ci + 1) * CHUNK]

    ltri = jnp.where(_tri(CHUNK), 1.0, 0.0).astype(F32)
    causal = _tri(CHUNK)
    strict = _tri(CHUNK, strict=True)
    onorm = onorm_ref[...]
    alog_row = alog_r[...]
    dtb_row = dtb_r[...]

    def chunk_body(ci, carry):
        rows = pl.ds(pl.multiple_of(ci * CHUNK, CHUNK), CHUNK)
        sm = sm_ref[rows, :]
        beta_w = _sigmoid(sm)
        g_w = -jnp.exp(alog_row) * _softplus(sm + dtb_row)
        gcol_w = _mm_hi(ltri, g_w)
        grow_c = grow_scr[ci]
        for h in range(HEADS):
            hc = slice(h * HEAD_DIM, (h + 1) * HEAD_DIM)
            qc = qkv_scr[rows, h * HEAD_DIM:(h + 1) * HEAD_DIM]
            kc = qkv_scr[rows, D_MODEL + h * HEAD_DIM:D_MODEL + (h + 1) * HEAD_DIM]
            vc = qkv_scr[rows, 2 * D_MODEL + h * HEAD_DIM:2 * D_MODEL + (h + 1) * HEAD_DIM]
            q = qc * (lax.rsqrt(jnp.sum(qc * qc, axis=-1, keepdims=True) + EPS) * (HEAD_DIM ** -0.5))
            k = kc * lax.rsqrt(jnp.sum(kc * kc, axis=-1, keepdims=True) + EPS)
            beta = beta_w[:, h:h + 1]
            g_c = gcol_w[:, 8 + h:9 + h]
            g_r = grow_c[8 + h:9 + h, :]
            g_last = g_r[:, CHUNK - 1:CHUNK]
            decay = jnp.where(causal, jnp.exp(jnp.where(causal, g_c - g_r, 0.0)), 0.0)
            eg = jnp.exp(g_c)
            kb = k * beta
            a_kk = jnp.where(strict, _mm_nt(kb, k) * decay, 0.0)
            xs = jnp.concatenate([vc * beta, kb * eg], axis=-1)
            p = -a_kk
            xs = xs + _mm_hi(p, xs)
            for _ in range(5):
                p = _mm_hi(p, p)
                xs = xs + _mm_hi(p, xs)
            u = xs[:, :HEAD_DIM]
            w = xs[:, HEAD_DIM:]
            a_qk = jnp.where(causal, _mm_nt(q, k) * decay, 0.0)
            s = s_scr[h]
            v_new = u - _mm(w, s)
            o = _mm(q * eg, s) + _mm(a_qk, v_new)
            s_scr[h] = s * jnp.exp(g_last) + _mm_tn(k * jnp.exp(g_last - g_c), v_new)
            gate = gate_ref[rows, hc]
            o_ref[rows, hc] = (_rms(o, onorm) * (gate * _sigmoid(gate))).astype(o_ref.dtype)
        return carry

    lax.fori_loop(0, nc, chunk_body, 0)


def _gdn(proj, small, small_t, conv, alog_r, dtb_r, alog_c, dtb_c, onorm, bsz, t, tblk=256):
    nt = bsz * t
    nb = t // tblk
    row = lambda b, i: b * nb + i
    blk = lambda col: pl.BlockSpec((tblk, D_MODEL), lambda b, i: (row(b, i), col))
    full = lambda a: pl.BlockSpec(a.shape, lambda b, i: (0,) * a.ndim)
    return pl.pallas_call(
        functools.partial(_gdn_kernel, tblk=tblk),
        grid=(bsz, nb),
        in_specs=[blk(0), blk(1), blk(2), blk(3),
                  pl.BlockSpec((tblk, 128), lambda b, i: (row(b, i), 0)),
                  pl.BlockSpec((16, tblk), lambda b, i: (0, row(b, i))),
                  full(conv), full(alog_r), full(dtb_r), full(alog_c), full(dtb_c), full(onorm)],
        out_specs=pl.BlockSpec((tblk, D_MODEL), lambda b, i: (row(b, i), 0)),
        out_shape=jax.ShapeDtypeStruct((nt, D_MODEL), BF16),
        scratch_shapes=[
            pltpu.VMEM((HEADS, HEAD_DIM, HEAD_DIM), F32),
            pltpu.VMEM((8, 3 * D_MODEL), F32),
            pltpu.VMEM((tblk, 3 * D_MODEL), F32),
            pltpu.VMEM((tblk // CHUNK, 16, CHUNK), F32),
        ],
        compiler_params=pltpu.CompilerParams(
            dimension_semantics=("arbitrary", "arbitrary"), vmem_limit_bytes=VMEM_LIMIT),
        name="gdn",
    )(proj, proj, proj, proj, small, small_t, conv, alog_r, dtb_r, alog_c, dtb_c, onorm)


def _hgrn_kernel(f_ref, i_ref, q_ref, gate_ref, lb_ref, onorm_ref, o_ref, st_scr, *, tblk):
    nc = tblk // CHUNK

    @pl.when(pl.program_id(1) == 0)
    def _():
        st_scr[...] = jnp.zeros_like(st_scr)

    lbp = lb_ref[...]
    e = jnp.exp(lbp - jnp.max(lbp, axis=0, keepdims=True))
    lb = e[0:1] / jnp.sum(e, axis=0, keepdims=True)
    ltri = jnp.where(_tri(CHUNK), 1.0, 0.0).astype(F32)
    onorm = onorm_ref[...]

    def chunk_body(ci, carry):
        rows = pl.ds(pl.multiple_of(ci * CHUNK, CHUNK), CHUNK)
        f_all = lb + (1.0 - lb) * _sigmoid(f_ref[rows, :])
        lf_all = jnp.log(f_all)
        b_all = _mm_hi(ltri, lf_all)
        for h in range(HEADS):
            hc = slice(h * HEAD_DIM, (h + 1) * HEAD_DIM)
            b = b_all[:, hc]
            lf = lf_all[:, hc]
            kk = 1.0 - f_all[:, hc]
            qr = q_ref[rows, hc]
            q = qr * _sigmoid(qr)
            v = i_ref[rows, hc]
            st = st_scr[h]
            b_last = b[CHUNK - 1:CHUNK, :]
            outs = []
            for blk in range(CHUNK // SUB):
                r0 = blk * SUB
                n = r0 + SUB
                ref = b[r0:r0 + 1, :] - lf[r0:r0 + 1, :]
                qt = q[r0:n] * jnp.exp(b[r0:n] - ref)
                kt = kk[:n] * jnp.exp(ref - b[:n])
                a = _mm_nt(qt, kt)
                rr = lax.broadcasted_iota(jnp.int32, (SUB, n), 0)
                cc = lax.broadcasted_iota(jnp.int32, (SUB, n), 1)
                a = jnp.where(cc <= rr + r0, a, 0.0)
                outs.append(_mm(a, v[:n]))
            o = jnp.concatenate(outs, axis=0) + _mm_nt(q * jnp.exp(b), st)
            st_scr[h] = st * jnp.exp(b_last) + _mm_tn(v, kk * jnp.exp(b_last - b))
            gate = gate_ref[rows, hc]
            o_ref[rows, hc] = (_rms(o, onorm) * _sigmoid(gate)).astype(o_ref.dtype)
        return carry

    lax.fori_loop(0, nc, chunk_body, 0)


def _hgrn(proj, lbp, onorm, bsz, t, tblk=256):
    nt = bsz * t
    nb = t // tblk
    row = lambda b, i: b * nb + i
    blk = lambda col: pl.BlockSpec((tblk, D_MODEL), lambda b, i: (row(b, i), col))
    full = lambda a: pl.BlockSpec(a.shape, lambda b, i: (0,) * a.ndim)
    return pl.pallas_call(
        functools.partial(_hgrn_kernel, tblk=tblk),
        grid=(bsz, nb),
        in_specs=[blk(4), blk(5), blk(6), blk(7), full(lbp), full(onorm)],
        out_specs=pl.BlockSpec((tblk, D_MODEL), lambda b, i: (row(b, i), 0)),
        out_shape=jax.ShapeDtypeStruct((nt, D_MODEL), BF16),
        scratch_shapes=[pltpu.VMEM((HEADS, HEAD_DIM, HEAD_DIM), F32)],
        compiler_params=pltpu.CompilerParams(
            dimension_semantics=("arbitrary", "arbitrary"), vmem_limit_bytes=VMEM_LIMIT),
        name="hgrn2",
    )(proj, proj, proj, proj, lbp, onorm)


def _cand_pairs():
    return [(a, b) for a in range(P_TOPK) for b in range(P_TOPK) if (a + 1) * (b + 1) <= N_CAND]


def _extract_top(s, n):
    tops = []
    cur = s
    for _ in range(n):
        m = jnp.max(cur, axis=0, keepdims=True)
        tops.append(m)
        cur = jnp.where(cur == m, -jnp.inf, cur)
    return tops


def _merge_kernel(oa_ref, ob_ref, ga_ref, gb_ref, x_ref, wa_ref, wb_ref, wo_ref, n2_ref,
                  wpq_ref, keys_ref, x1_ref, ht_ref, e1_ref, e2_ref, th_ref):
    ya = jnp.dot(oa_ref[...], wa_ref[...], preferred_element_type=F32)
    yb = jnp.dot(ob_ref[...], wb_ref[...], preferred_element_type=F32)
    mix = _sigmoid(ga_ref[...]) * ya + _sigmoid(gb_ref[...]) * yb
    x1 = x_ref[...] + _mm(mix, wo_ref[...])
    x1_ref[...] = x1
    h2 = _rms(x1, n2_ref[...])
    ht_ref[...] = h2.T.astype(BF16)
    qp = _mm(h2, wpq_ref[...])
    pairs = _cand_pairs()
    for h in range(P_HEADS):
        tops = []
        scores = []
        for p in range(2):
            c0 = h * 2 * HEAD_DIM + p * HEAD_DIM
            s = _mm_nt(keys_ref[p, h], qp[:, c0:c0 + HEAD_DIM])
            scores.append(s)
            tops.append(_extract_top(s, P_TOPK))
        cand = jnp.concatenate([tops[0][a] + tops[1][b] for a, b in pairs], axis=0)
        ctop = _extract_top(cand, N_CAND)
        c_max = ctop[0]
        z = ctop[0] * 0.0
        for r_ in range(P_TOPK):
            z = z + jnp.exp(ctop[r_] - c_max)
        inv_z = 1.0 / z
        s1, s2 = scores
        e1_ref[h] = jnp.where(s1 >= tops[0][P_TOPK - 1], jnp.exp(s1 - tops[0][0]), 0.0)
        e2_ref[h] = jnp.where(s2 >= tops[1][P_TOPK - 1], jnp.exp(s2 - tops[1][0]), 0.0) * inv_z
        th_ref[h:h + 1, :] = jnp.exp(0.5 * (ctop[P_TOPK - 1] + ctop[P_TOPK]) - c_max) * inv_z


def _merge(oa, ob, proj, x2, wa, wb, wo, n2, wpq, keys, tb=256):
    nt = x2.shape[0]
    rowblk = lambda col: pl.BlockSpec((tb, D_MODEL), lambda i: (i, col))
    full = lambda a: pl.BlockSpec(a.shape, lambda i: (0,) * a.ndim)
    return pl.pallas_call(
        _merge_kernel,
        grid=(nt // tb,),
        in_specs=[rowblk(0), rowblk(0), rowblk(8), rowblk(9), rowblk(0),
                  full(wa), full(wb), full(wo), full(n2), full(wpq), full(keys)],
        out_specs=[
            pl.BlockSpec((tb, D_MODEL), lambda i: (i, 0)),
            pl.BlockSpec((D_MODEL, tb), lambda i: (0, i)),
            pl.BlockSpec((P_HEADS, N_KEYS, tb), lambda i: (0, 0, i)),
            pl.BlockSpec((P_HEADS, N_KEYS, tb), lambda i: (0, 0, i)),
            pl.BlockSpec((P_HEADS, tb), lambda i: (0, i)),
        ],
        out_shape=[
            jax.ShapeDtypeStruct((nt, D_MODEL), F32),
            jax.ShapeDtypeStruct((D_MODEL, nt), BF16),
            jax.ShapeDtypeStruct((P_HEADS, N_KEYS, nt), F32),
            jax.ShapeDtypeStruct((P_HEADS, N_KEYS, nt), F32),
            jax.ShapeDtypeStruct((P_HEADS, nt), F32),
        ],
        compiler_params=pltpu.CompilerParams(
            dimension_semantics=("arbitrary",), vmem_limit_bytes=VMEM_LIMIT),
        name="merge",
    )(oa, ob, proj, proj, x2, wa, wb, wo, n2, wpq, keys)


def _peer_kernel(ht_ref, e1_ref, e2_ref, th_ref, u_ref, vt_ref, x1_ref, fn_ref, o_ref, acc_scr,
                 *, eb):
    j = pl.program_id(1)

    @pl.when(j == 0)
    def _():
        acc_scr[...] = jnp.zeros_like(acc_scr)

    ht = ht_ref[...]
    for ii in range(eb // N_KEYS):
        i1 = j * (eb // N_KEYS) + ii
        hid = jnp.dot(u_ref[ii * N_KEYS:(ii + 1) * N_KEYS, :], ht, preferred_element_type=F32)
        wgt = jnp.zeros_like(hid)
        for h in range(P_HEADS):
            p = e2_ref[h] * e1_ref[h, pl.ds(i1, 1), :]
            wgt = wgt + jnp.where(p >= th_ref[h:h + 1, :], p, 0.0)
        act = 0.5 * hid * (1.0 + lax.erf(hid * (2.0 ** -0.5)))
        acc_scr[...] += jnp.dot(vt_ref[:, ii * N_KEYS:(ii + 1) * N_KEYS], (wgt * act).astype(BF16),
                                preferred_element_type=F32)

    @pl.when(j == pl.num_programs(1) - 1)
    def _():
        o_ref[...] = _rms(x1_ref[...] + acc_scr[...].T, fn_ref[...])


def _peer(ht, e1, e2, th, u, vt, x1, fn, tb=512, eb=1024):
    nt = x1.shape[0]
    ne = u.shape[0]
    return pl.pallas_call(
        functools.partial(_peer_kernel, eb=eb),
        grid=(nt // tb, ne // eb),
        in_specs=[
            pl.BlockSpec((D_MODEL, tb), lambda i, j: (0, i)),
            pl.BlockSpec((P_HEADS, N_KEYS, tb), lambda i, j: (0, 0, i)),
            pl.BlockSpec((P_HEADS, N_KEYS, tb), lambda i, j: (0, 0, i)),
            pl.BlockSpec((P_HEADS, tb), lambda i, j: (0, i)),
            pl.BlockSpec((eb, D_MODEL), lambda i, j: (j, 0)),
            pl.BlockSpec((D_MODEL, eb), lambda i, j: (0, j)),
            pl.BlockSpec((tb, D_MODEL), lambda i, j: (i, 0)),
            pl.BlockSpec((1, D_MODEL), lambda i, j: (0, 0)),
        ],
        out_specs=pl.BlockSpec((tb, D_MODEL), lambda i, j: (i, 0)),
        out_shape=jax.ShapeDtypeStruct((nt, D_MODEL), F32),
        scratch_shapes=[pltpu.VMEM((D_MODEL, tb), F32)],
        compiler_params=pltpu.CompilerParams(
            dimension_semantics=("arbitrary", "arbitrary"), vmem_limit_bytes=VMEM_LIMIT),
        name="peer",
    )(ht, e1, e2, th, u, vt, x1, fn)


def _pad_lanes(row, offset):
    return jnp.zeros((1, 128), F32).at[0, offset:offset + row.shape[0]].set(row.astype(F32))


def kernel(x, norm1, w_in, conv_a, a_log, dt_bias, a_onorm, b_lower_bound, b_onorm, w_branch_a,
           w_branch_b, w_out, norm2, w_pq, sub_keys, expert_u, expert_v, final_norm):
    assert norm1.shape[0] == 1, "one layer"
    bsz, t, _ = x.shape
    nt = bsz * t
    x2 = x.reshape(nt, D_MODEL)

    wi = w_in[0]
    n_qkv = 3 * D_MODEL
    n_used = 10 * D_MODEL + 2 * HEADS
    w_main = jnp.concatenate([wi[:, :n_qkv], wi[:, n_qkv + 2 * HEADS:n_used]], axis=1).astype(BF16)
    w_small = jnp.pad(wi[:, n_qkv:n_qkv + 2 * HEADS], ((0, 0), (0, 128 - 2 * HEADS))).astype(BF16)
    w_small_t = wi[:, n_qkv:n_qkv + 2 * HEADS].T.astype(BF16)

    proj, small, small_t = _inproj(x2, norm1[0][None, :], w_main, w_small, w_small_t)

    alog_r = _pad_lanes(a_log[0], HEADS)
    dtb_r = _pad_lanes(dt_bias[0], HEADS)
    alog_c = alog_r[0, :16][:, None]
    dtb_c = dtb_r[0, :16][:, None]
    oa = _gdn(proj, small, small_t, conv_a[0], alog_r, dtb_r, alog_c, dtb_c,
              a_onorm[0][None, :], bsz, t)
    ob = _hgrn(proj, b_lower_bound, b_onorm[0][None, :], bsz, t)

    x1, ht, e1, e2, th = _merge(
        oa, ob, proj, x2, w_branch_a[0].astype(BF16), w_branch_b[0].astype(BF16),
        w_out[0].astype(BF16), norm2[0][None, :], w_pq[0].astype(BF16), sub_keys[0].astype(BF16))

    out = _peer(ht, e1, e2, th, expert_u[0].astype(BF16), expert_v[0].T.astype(BF16), x1,
                final_norm[None, :])
    return out.reshape(bsz, t, D_MODEL)
```

```python
import functools

import jax
import jax.numpy as jnp
from jax import lax
from jax.experimental import pallas as pl
from jax.experimental.pallas import tpu as pltpu

F32 = jnp.float32
BF16 = jnp.bfloat16
EPS = 1e-6

D_MODEL = 1024
HEADS = 8
HEAD_DIM = 128
CHUNK = 64
CONV_W = 4
SUB = 16

N_KEYS = 128
P_HEADS = 8
P_TOPK = 16
RANK_NONE = 64.0
MXU_DEPTH = 256

VMEM_LIMIT = 56 * 1024 * 1024

_NT = (((1,), (1,)), ((), ()))
_TN = (((0,), (0,)), ((), ()))


def _mm(a, b):
    return jnp.dot(a.astype(BF16), b.astype(BF16), preferred_element_type=F32)


def _mm_nt(a, b):
    return lax.dot_general(a.astype(BF16), b.astype(BF16), _NT, preferred_element_type=F32)


def _mm_tn(a, b):
    return lax.dot_general(a.astype(BF16), b.astype(BF16), _TN, preferred_element_type=F32)


def _split(a):
    hi = a.astype(BF16)
    return hi, (a - hi.astype(F32)).astype(BF16)


def _mm_solve(a, b):
    a_hi, a_lo = _split(a)
    b_hi, b_lo = _split(b)
    dot = functools.partial(jnp.dot, preferred_element_type=F32)
    return dot(a_hi, b_hi) + (dot(a_hi, b_lo) + dot(a_lo, b_hi))


def _split3(b):
    b_hi, b_lo = _split(b)
    return b_hi, b_lo, (b - b_hi.astype(F32) - b_lo.astype(F32)).astype(BF16)


def _mm_cumsum(tri01, b):
    b_hi, b_lo, b_lo2 = _split3(b)
    t = tri01.astype(BF16)
    dot = functools.partial(jnp.dot, preferred_element_type=F32)
    return dot(t, b_hi) + (dot(t, b_lo) + dot(t, b_lo2))


def _mm_cumsum_rhs(a, tri01):
    a_hi, a_lo, a_lo2 = _split3(a)
    t = tri01.astype(BF16)
    dot = functools.partial(jnp.dot, preferred_element_type=F32)
    return dot(a_hi, t) + (dot(a_lo, t) + dot(a_lo2, t))


def _sigmoid(x):
    return 1.0 / (1.0 + jnp.exp(-x))


def _softplus(x):
    return jnp.maximum(x, 0.0) + jnp.log(1.0 + jnp.exp(-jnp.abs(x)))


def _rms(x, g):
    return x * lax.rsqrt(jnp.mean(x * x, axis=-1, keepdims=True) + EPS) * g


def _tri(n, strict=False):
    r = lax.broadcasted_iota(jnp.int32, (n, n), 0)
    c = lax.broadcasted_iota(jnp.int32, (n, n), 1)
    return (r > c) if strict else (r >= c)


def _inproj_kernel(x_ref, g_ref, w_ref, ws_ref, wst_ref, o_ref, os_ref, ost_ref, h_scr):
    @pl.when(pl.program_id(1) == 0)
    def _():
        hb = _rms(x_ref[...], g_ref[...]).astype(BF16)
        h_scr[...] = hb
        os_ref[...] = jnp.dot(hb, ws_ref[...], preferred_element_type=F32)
        ost_ref[...] = lax.dot_general(wst_ref[...], hb, _NT, preferred_element_type=F32)

    o_ref[...] = jnp.dot(h_scr[...], w_ref[...], preferred_element_type=F32)


def _inproj(x2, g, w_main, w_small, w_small_t, tm=1024, tn=1024):
    nt = x2.shape[0]
    ncol = w_main.shape[1]
    return pl.pallas_call(
        _inproj_kernel,
        grid=(nt // tm, ncol // tn),
        in_specs=[
            pl.BlockSpec((tm, D_MODEL), lambda i, j: (i, 0)),
            pl.BlockSpec((1, D_MODEL), lambda i, j: (0, 0)),
            pl.BlockSpec((D_MODEL, tn), lambda i, j: (0, j)),
            pl.BlockSpec((D_MODEL, 128), lambda i, j: (0, 0)),
            pl.BlockSpec((16, D_MODEL), lambda i, j: (0, 0)),
        ],
        out_specs=[
            pl.BlockSpec((tm, tn), lambda i, j: (i, j)),
            pl.BlockSpec((tm, 128), lambda i, j: (i, 0)),
            pl.BlockSpec((16, tm), lambda i, j: (0, i)),
        ],
        out_shape=[
            jax.ShapeDtypeStruct((nt, ncol), F32),
            jax.ShapeDtypeStruct((nt, 128), F32),
            jax.ShapeDtypeStruct((16, nt), F32),
        ],
        scratch_shapes=[pltpu.VMEM((tm, D_MODEL), BF16)],
        compiler_params=pltpu.CompilerParams(
            dimension_semantics=("arbitrary", "arbitrary"), vmem_limit_bytes=VMEM_LIMIT),
        name="inproj",
    )(x2, g, w_main, w_small, w_small_t)


def _gdn_kernel(q_ref, k_ref, v_ref, gate_ref, sm_ref, smt_ref, conv_ref, alog_r, dtb_r,
                alog_c, dtb_c, onorm_ref, o_ref, s_scr, tail_scr, qkv_scr, grow_scr, *, tblk):
    nc = tblk // CHUNK

    @pl.when(pl.program_id(1) == 0)
    def _():
        s_scr[...] = jnp.zeros_like(s_scr)
        tail_scr[...] = jnp.zeros_like(tail_scr)

    for idx, ref in enumerate((q_ref, k_ref, v_ref)):
        cols = slice(idx * D_MODEL, (idx + 1) * D_MODEL)
        x = ref[...]
        xc = jnp.concatenate([tail_scr[:, cols], x], axis=0)
        w = conv_ref[:, cols]
        y = x * w[CONV_W - 1:CONV_W]
        for j in range(CONV_W - 1):
            off = 8 - (CONV_W - 1) + j
            y = y + xc[off:off + tblk] * w[j:j + 1]
        qkv_scr[:, cols] = y * _sigmoid(y)
        tail_scr[:, cols] = x[tblk - 8:]

    g_t = -jnp.exp(alog_c[...]) * _softplus(smt_ref[...] + dtb_c[...])
    r = lax.broadcasted_iota(jnp.int32, (tblk, tblk), 0)
    c = lax.broadcasted_iota(jnp.int32, (tblk, tblk), 1)
    lg_chunk = CHUNK.bit_length() - 1
    same_chunk = jnp.right_shift(r, lg_chunk) == jnp.right_shift(c, lg_chunk)
    ublk = jnp.where(r <= c, jnp.where(same_chunk, 1.0, 0.0), 0.0).astype(F32)
    grow = _mm_cumsum_rhs(g_t, ublk)
    for ci in range(nc):
        grow_scr[ci] = grow[:, ci * CHUNK:(ci + 1) * CHUNK]

    ltri = jnp.where(_tri(CHUNK), 1.0, 0.0).astype(F32)
    causal = _tri(CHUNK)
    strict = _tri(CHUNK, strict=True)
    onorm = onorm_ref[...]
    alog_row = alog_r[...]
    dtb_row = dtb_r[...]

    def chunk_body(ci, carry):
        rows = pl.ds(pl.multiple_of(ci * CHUNK, CHUNK), CHUNK)
        sm = sm_ref[rows, :]
        beta_w = _sigmoid(sm)
        g_w = -jnp.exp(alog_row) * _softplus(sm + dtb_row)
        gcol_w = _mm_cumsum(ltri, g_w)
        grow_c = grow_scr[ci]
        hs = range(HEADS)
        hcols = [slice(h * HEAD_DIM, (h + 1) * HEAD_DIM) for h in hs]

        p, xs, a_qk, q_dec, k_dec, g_last = [], [], [], [], [], []
        for h in hs:
            qc = qkv_scr[rows, h * HEAD_DIM:(h + 1) * HEAD_DIM]
            kc = qkv_scr[rows, D_MODEL + h * HEAD_DIM:D_MODEL + (h + 1) * HEAD_DIM]
            vc = qkv_scr[rows, 2 * D_MODEL + h * HEAD_DIM:2 * D_MODEL + (h + 1) * HEAD_DIM]
            q = qc * (lax.rsqrt(jnp.sum(qc * qc, axis=-1, keepdims=True) + EPS) * (HEAD_DIM ** -0.5))
            k = kc * lax.rsqrt(jnp.sum(kc * kc, axis=-1, keepdims=True) + EPS)
            beta = beta_w[:, h:h + 1]
            g_c = gcol_w[:, HEADS + h:HEADS + h + 1]
            g_r = grow_c[HEADS + h:HEADS + h + 1, :]
            gl = g_r[:, CHUNK - 1:CHUNK]
            decay = jnp.where(causal, jnp.exp(jnp.where(causal, g_c - g_r, 0.0)), 0.0)
            eg = jnp.exp(g_c)
            kb = k * beta
            p.append(-jnp.where(strict, _mm_nt(kb, k) * decay, 0.0))
            a_qk.append(jnp.where(causal, _mm_nt(q, k) * decay, 0.0))
            xs.append(jnp.concatenate([vc * beta, kb * eg], axis=-1))
            q_dec.append(q * eg)
            k_dec.append(k * jnp.exp(gl - g_c))
            g_last.append(jnp.exp(gl))

        n_lvl = CHUNK.bit_length() - 1
        for lvl in range(n_lvl):
            last = lvl == n_lvl - 1
            for h in hs:
                rhs = xs[h] if last else jnp.concatenate([xs[h], p[h]], axis=-1)
                r = _mm_solve(p[h], rhs)
                xs[h] = xs[h] + r[:, :2 * HEAD_DIM]
                if not last:
                    p[h] = r[:, 2 * HEAD_DIM:]

        s_old = [s_scr[h] for h in hs]
        v_new = [xs[h][:, :HEAD_DIM] - _mm(xs[h][:, HEAD_DIM:], s_old[h]) for h in hs]
        for h in hs:
            o = _mm(q_dec[h], s_old[h]) + _mm(a_qk[h], v_new[h])
            s_scr[h] = s_old[h] * g_last[h] + _mm_tn(k_dec[h], v_new[h])
            gate = gate_ref[rows, hcols[h]]
            o_ref[rows, hcols[h]] = (_rms(o, onorm) * (gate * _sigmoid(gate))).astype(o_ref.dtype)
        return carry

    lax.fori_loop(0, nc, chunk_body, 0)


def _gdn(proj, small, small_t, conv, alog_r, dtb_r, alog_c, dtb_c, onorm, bsz, t, tblk=256):
    nt = bsz * t
    nb = t // tblk
    row = lambda b, i: b * nb + i
    blk = lambda col: pl.BlockSpec((tblk, D_MODEL), lambda b, i: (row(b, i), col))
    full = lambda a: pl.BlockSpec(a.shape, lambda b, i: (0,) * a.ndim)
    return pl.pallas_call(
        functools.partial(_gdn_kernel, tblk=tblk),
        grid=(bsz, nb),
        in_specs=[blk(0), blk(1), blk(2), blk(3),
                  pl.BlockSpec((tblk, 128), lambda b, i: (row(b, i), 0)),
                  pl.BlockSpec((16, tblk), lambda b, i: (0, row(b, i))),
                  full(conv), full(alog_r), full(dtb_r), full(alog_c), full(dtb_c), full(onorm)],
        out_specs=pl.BlockSpec((tblk, D_MODEL), lambda b, i: (row(b, i), 0)),
        out_shape=jax.ShapeDtypeStruct((nt, D_MODEL), BF16),
        scratch_shapes=[
            pltpu.VMEM((HEADS, HEAD_DIM, HEAD_DIM), F32),
            pltpu.VMEM((8, 3 * D_MODEL), F32),
            pltpu.VMEM((tblk, 3 * D_MODEL), F32),
            pltpu.VMEM((tblk // CHUNK, 16, CHUNK), F32),
        ],
        compiler_params=pltpu.CompilerParams(
            dimension_semantics=("arbitrary", "arbitrary"), vmem_limit_bytes=VMEM_LIMIT),
        name="gdn",
    )(proj, proj, proj, proj, small, small_t, conv, alog_r, dtb_r, alog_c, dtb_c, onorm)


def _hgrn_kernel(f_ref, i_ref, q_ref, gate_ref, lb_ref, onorm_ref, o_ref, st_scr, *, tblk):
    nc = tblk // CHUNK

    @pl.when(pl.program_id(1) == 0)
    def _():
        st_scr[...] = jnp.zeros_like(st_scr)

    lbp = lb_ref[...]
    e = jnp.exp(lbp - jnp.max(lbp, axis=0, keepdims=True))
    lb = e[0:1] / jnp.sum(e, axis=0, keepdims=True)
    ltri = jnp.where(_tri(CHUNK), 1.0, 0.0).astype(F32)
    onorm = onorm_ref[...]

    def chunk_body(ci, carry):
        rows = pl.ds(pl.multiple_of(ci * CHUNK, CHUNK), CHUNK)
        f_all = lb + (1.0 - lb) * _sigmoid(f_ref[rows, :])
        lf_all = jnp.log(f_all)
        b_all = _mm_cumsum(ltri, lf_all)
        hs = range(HEADS)
        hcols = [slice(h * HEAD_DIM, (h + 1) * HEAD_DIM) for h in hs]
        q_all = q_ref[rows, :]
        q_all = q_all * _sigmoid(q_all)
        k_all = 1.0 - f_all
        eb_all = jnp.exp(b_all)
        n_sub = CHUNK // SUB

        a_blk = []
        for h in hs:
            b = b_all[:, hcols[h]]
            lf = lf_all[:, hcols[h]]
            kk = k_all[:, hcols[h]]
            q = q_all[:, hcols[h]]
            blks = []
            for blk in range(n_sub):
                r0 = blk * SUB
                n = r0 + SUB
                ref = b[r0:r0 + 1, :] - lf[r0:r0 + 1, :]
                qt = q[r0:n] * jnp.exp(b[r0:n] - ref)
                kt = kk[:n] * jnp.exp(ref - b[:n])
                a = _mm_nt(qt, kt)
                rr = lax.broadcasted_iota(jnp.int32, (SUB, n), 0)
                cc = lax.broadcasted_iota(jnp.int32, (SUB, n), 1)
                blks.append(jnp.where(cc <= rr + r0, a, 0.0))
            a_blk.append(blks)
        st_old = [st_scr[h] for h in hs]
        outs = []
        for h in hs:
            v = i_ref[rows, hcols[h]]
            intra = [_mm(a_blk[h][blk], v[:(blk + 1) * SUB]) for blk in range(n_sub)]
            inter = _mm_nt(q_all[:, hcols[h]] * eb_all[:, hcols[h]], st_old[h])
            outs.append(jnp.concatenate(intra, axis=0) + inter)
        for h in hs:
            b = b_all[:, hcols[h]]
            b_last = b[CHUNK - 1:CHUNK, :]
            v = i_ref[rows, hcols[h]]
            st_scr[h] = (st_old[h] * jnp.exp(b_last)
                         + _mm_tn(v, k_all[:, hcols[h]] * jnp.exp(b_last - b)))
            gate = gate_ref[rows, hcols[h]]
            o_ref[rows, hcols[h]] = (_rms(outs[h], onorm) * _sigmoid(gate)).astype(o_ref.dtype)
        return carry

    lax.fori_loop(0, nc, chunk_body, 0)


def _hgrn(proj, lbp, onorm, bsz, t, tblk=256):
    nt = bsz * t
    nb = t // tblk
    row = lambda b, i: b * nb + i
    blk = lambda col: pl.BlockSpec((tblk, D_MODEL), lambda b, i: (row(b, i), col))
    full = lambda a: pl.BlockSpec(a.shape, lambda b, i: (0,) * a.ndim)
    return pl.pallas_call(
        functools.partial(_hgrn_kernel, tblk=tblk),
        grid=(bsz, nb),
        in_specs=[blk(4), blk(5), blk(6), blk(7), full(lbp), full(onorm)],
        out_specs=pl.BlockSpec((tblk, D_MODEL), lambda b, i: (row(b, i), 0)),
        out_shape=jax.ShapeDtypeStruct((nt, D_MODEL), BF16),
        scratch_shapes=[pltpu.VMEM((HEADS, HEAD_DIM, HEAD_DIM), F32)],
        compiler_params=pltpu.CompilerParams(
            dimension_semantics=("arbitrary", "arbitrary"), vmem_limit_bytes=VMEM_LIMIT),
        name="hgrn2",
    )(proj, proj, proj, proj, lbp, onorm)


def _cand_pairs():
    return [(a, b) for a in range(P_TOPK) for b in range(P_TOPK) if (a + 1) * (b + 1) <= P_TOPK]


def _extract_top(s, n, with_rank=False):
    tops = []
    cur = s
    rank = jnp.full(s.shape, RANK_NONE, F32) if with_rank else None
    for r in range(n):
        m = jnp.max(cur, axis=0, keepdims=True)
        tops.append(m)
        hit = cur == m
        if with_rank:
            rank = jnp.where(hit, float(r), rank)
        cur = jnp.where(hit, -jnp.inf, cur)
    return (tops, rank) if with_rank else tops


def _merge_kernel(oa_ref, ob_ref, ga_ref, gb_ref, x_ref, wa_ref, wb_ref, wo_ref, n2_ref,
                  wpq_ref, keys_ref, x1_ref, ht_ref, e1_ref, cnt_ref, e2_ref, rk_ref):
    ya = jnp.dot(oa_ref[...], wa_ref[...], preferred_element_type=F32)
    yb = jnp.dot(ob_ref[...], wb_ref[...], preferred_element_type=F32)
    mix = _sigmoid(ga_ref[...]) * ya + _sigmoid(gb_ref[...]) * yb
    x1 = x_ref[...] + _mm(mix, wo_ref[...])
    x1_ref[...] = x1
    h2 = _rms(x1, n2_ref[...])
    ht_ref[...] = h2.T.astype(BF16)
    qp = _mm(h2, wpq_ref[...])
    pairs = _cand_pairs()
    for h in range(P_HEADS):
        c0 = h * 2 * HEAD_DIM
        s1 = _mm_nt(keys_ref[0, h], qp[:, c0:c0 + HEAD_DIM])
        s2 = _mm_nt(keys_ref[1, h], qp[:, c0 + HEAD_DIM:c0 + 2 * HEAD_DIM])
        top1 = _extract_top(s1, P_TOPK)
        top2, rank2 = _extract_top(s2, P_TOPK, with_rank=True)
        cand = jnp.concatenate([top1[a] + top2[b] for a, b in pairs], axis=0)
        ctop = _extract_top(cand, P_TOPK)
        c_max = ctop[0]
        tau = ctop[P_TOPK - 1]
        z = jnp.exp(ctop[0] - c_max)
        for r_ in range(1, P_TOPK):
            z = z + jnp.exp(ctop[r_] - c_max)
        inv_z = 1.0 / z
        in1 = s1 >= top1[P_TOPK - 1]
        cnt = jnp.zeros_like(s1)
        for r2 in range(P_TOPK):
            cnt = cnt + jnp.where(s1 + top2[r2] >= tau, 1.0, 0.0)
        e1_ref[h] = jnp.where(in1, jnp.exp(s1 - top1[0]), 0.0)
        cnt_ref[h] = jnp.where(in1, cnt, 0.0)
        e2_ref[h] = (jnp.where(rank2 < RANK_NONE, jnp.exp(s2 - top2[0]), 0.0) * inv_z).astype(BF16)
        rk_ref[h] = rank2.astype(BF16)


def _merge(oa, ob, proj, x2, wa, wb, wo, n2, wpq, keys, tb=256):
    nt = x2.shape[0]
    rowblk = lambda col: pl.BlockSpec((tb, D_MODEL), lambda i: (i, col))
    full = lambda a: pl.BlockSpec(a.shape, lambda i: (0,) * a.ndim)
    return pl.pallas_call(
        _merge_kernel,
        grid=(nt // tb,),
        in_specs=[rowblk(0), rowblk(0), rowblk(8), rowblk(9), rowblk(0),
                  full(wa), full(wb), full(wo), full(n2), full(wpq), full(keys)],
        out_specs=[
            pl.BlockSpec((tb, D_MODEL), lambda i: (i, 0)),
            pl.BlockSpec((D_MODEL, tb), lambda i: (0, i)),
            pl.BlockSpec((P_HEADS, N_KEYS, tb), lambda i: (0, 0, i)),
            pl.BlockSpec((P_HEADS, N_KEYS, tb), lambda i: (0, 0, i)),
            pl.BlockSpec((P_HEADS, N_KEYS, tb), lambda i: (0, 0, i)),
            pl.BlockSpec((P_HEADS, N_KEYS, tb), lambda i: (0, 0, i)),
        ],
        out_shape=[
            jax.ShapeDtypeStruct((nt, D_MODEL), F32),
            jax.ShapeDtypeStruct((D_MODEL, nt), BF16),
            jax.ShapeDtypeStruct((P_HEADS, N_KEYS, nt), F32),
            jax.ShapeDtypeStruct((P_HEADS, N_KEYS, nt), F32),
            jax.ShapeDtypeStruct((P_HEADS, N_KEYS, nt), BF16),
            jax.ShapeDtypeStruct((P_HEADS, N_KEYS, nt), BF16),
        ],
        compiler_params=pltpu.CompilerParams(
            dimension_semantics=("arbitrary",), vmem_limit_bytes=VMEM_LIMIT),
        name="merge",
    )(oa, ob, proj, proj, x2, wa, wb, wo, n2, wpq, keys)


def _peer_kernel(ht_ref, e1_ref, cnt_ref, e2_ref, rk_ref, u_ref, vt_ref, x1_ref, fn_ref, o_ref,
                 acc_scr, *, eb):
    j = pl.program_id(1)
    tb = ht_ref.shape[1]

    @pl.when(j == 0)
    def _():
        acc_scr[...] = jnp.zeros_like(acc_scr)

    ht = ht_ref[...]
    per = MXU_DEPTH // N_KEYS
    n_grp = eb // MXU_DEPTH
    grp_rows = [slice(g * MXU_DEPTH, (g + 1) * MXU_DEPTH) for g in range(n_grp)]
    hid_next = jnp.dot(u_ref[grp_rows[0], :], ht, preferred_element_type=F32)
    for g in range(n_grp):
        rows = grp_rows[g]
        hid = hid_next
        if g + 1 < n_grp:
            hid_next = jnp.dot(u_ref[grp_rows[g + 1], :], ht, preferred_element_type=F32)
        parts = []
        for sub in range(per):
            i1 = j * (eb // N_KEYS) + g * per + sub
            wgt = jnp.zeros((N_KEYS, tb), BF16)
            for h in range(P_HEADS):
                e1 = e1_ref[h, pl.ds(i1, 1), :].astype(BF16)
                cnt = cnt_ref[h, pl.ds(i1, 1), :].astype(BF16)
                wgt = wgt + jnp.where(rk_ref[h] < cnt, e2_ref[h] * e1, jnp.zeros((), BF16))
            parts.append(wgt)
        wgt = jnp.concatenate(parts, axis=0)
        act = (0.5 * hid * (1.0 + lax.erf(hid * (2.0 ** -0.5)))).astype(BF16)
        acc_scr[...] += jnp.dot(vt_ref[:, rows], wgt * act, preferred_element_type=F32)

    @pl.when(j == pl.num_programs(1) - 1)
    def _():
        o_ref[...] = _rms(x1_ref[...] + acc_scr[...].T, fn_ref[...])


def _peer(ht, e1, cnt, e2, rk, u, vt, x1, fn, tb=512, eb=2048):
    nt = x1.shape[0]
    ne = u.shape[0]
    return pl.pallas_call(
        functools.partial(_peer_kernel, eb=eb),
        grid=(nt // tb, ne // eb),
        in_specs=[
            pl.BlockSpec((D_MODEL, tb), lambda i, j: (0, i)),
            pl.BlockSpec((P_HEADS, N_KEYS, tb), lambda i, j: (0, 0, i)),
            pl.BlockSpec((P_HEADS, N_KEYS, tb), lambda i, j: (0, 0, i)),
            pl.BlockSpec((P_HEADS, N_KEYS, tb), lambda i, j: (0, 0, i)),
            pl.BlockSpec((P_HEADS, N_KEYS, tb), lambda i, j: (0, 0, i)),
            pl.BlockSpec((eb, D_MODEL), lambda i, j: (j, 0)),
            pl.BlockSpec((D_MODEL, eb), lambda i, j: (0, j)),
            pl.BlockSpec((tb, D_MODEL), lambda i, j: (i, 0)),
            pl.BlockSpec((1, D_MODEL), lambda i, j: (0, 0)),
        ],
        out_specs=pl.BlockSpec((tb, D_MODEL), lambda i, j: (i, 0)),
        out_shape=jax.ShapeDtypeStruct((nt, D_MODEL), F32),
        scratch_shapes=[pltpu.VMEM((D_MODEL, tb), F32)],
        compiler_params=pltpu.CompilerParams(
            dimension_semantics=("arbitrary", "arbitrary"), vmem_limit_bytes=VMEM_LIMIT),
        name="peer",
    )(ht, e1, cnt, e2, rk, u, vt, x1, fn)


def _pad_lanes(row, offset):
    return jnp.zeros((1, 128), F32).at[0, offset:offset + row.shape[0]].set(row.astype(F32))


def kernel(x, norm1, w_in, conv_a, a_log, dt_bias, a_onorm, b_lower_bound, b_onorm, w_branch_a,
           w_branch_b, w_out, norm2, w_pq, sub_keys, expert_u, expert_v, final_norm):
    assert norm1.shape[0] == 1, "one layer"
    bsz, t, _ = x.shape
    nt = bsz * t
    x2 = x.reshape(nt, D_MODEL)

    wi = w_in[0]
    n_qkv = 3 * D_MODEL
    n_used = 10 * D_MODEL + 2 * HEADS
    w_main = jnp.concatenate([wi[:, :n_qkv], wi[:, n_qkv + 2 * HEADS:n_used]], axis=1).astype(BF16)
    w_small = jnp.pad(wi[:, n_qkv:n_qkv + 2 * HEADS], ((0, 0), (0, 128 - 2 * HEADS))).astype(BF16)
    w_small_t = wi[:, n_qkv:n_qkv + 2 * HEADS].T.astype(BF16)

    proj, small, small_t = _inproj(x2, norm1[0][None, :], w_main, w_small, w_small_t)

    alog_r = _pad_lanes(a_log[0], HEADS)
    dtb_r = _pad_lanes(dt_bias[0], HEADS)
    alog_c = alog_r[0, :16][:, None]
    dtb_c = dtb_r[0, :16][:, None]
    oa = _gdn(proj, small, small_t, conv_a[0], alog_r, dtb_r, alog_c, dtb_c,
              a_onorm[0][None, :], bsz, t)
    ob = _hgrn(proj, b_lower_bound, b_onorm[0][None, :], bsz, t)

    x1, ht, e1, cnt, e2, rk = _merge(
        oa, ob, proj, x2, w_branch_a[0].astype(BF16), w_branch_b[0].astype(BF16),
        w_out[0].astype(BF16), norm2[0][None, :], w_pq[0].astype(BF16), sub_keys[0].astype(BF16))

    out = _peer(ht, e1, cnt, e2, rk, expert_u[0].astype(BF16), expert_v[0].T.astype(BF16), x1,
                final_norm[None, :])
    return out.reshape(bsz, t, D_MODEL)
```

```python
import functools

import jax
import jax.numpy as jnp
from jax import lax
from jax.experimental import pallas as pl
from jax.experimental.pallas import tpu as pltpu

F32 = jnp.float32
BF16 = jnp.bfloat16
EPS = 1e-6

D_MODEL = 1024
HEADS = 8
HEAD_DIM = 128
CHUNK = 64
CONV_W = 4
PAD_ROWS = 8
SUB = 16

N_KEYS = 128
P_HEADS = 8
P_TOPK = 16
RANK_NONE = 64.0
MXU_DEPTH = 256

VMEM_LIMIT = 56 * 1024 * 1024

_NT = (((1,), (1,)), ((), ()))
_TN = (((0,), (0,)), ((), ()))


def _mm(a, b):
    return jnp.dot(a.astype(BF16), b.astype(BF16), preferred_element_type=F32)


def _mm_nt(a, b):
    return lax.dot_general(a.astype(BF16), b.astype(BF16), _NT, preferred_element_type=F32)


def _mm_tn(a, b):
    return lax.dot_general(a.astype(BF16), b.astype(BF16), _TN, preferred_element_type=F32)


def _split(a):
    hi = a.astype(BF16)
    return hi, (a - hi.astype(F32)).astype(BF16)


def _mm_solve(a, b):
    a_hi, a_lo = _split(a)
    b_hi, b_lo = _split(b)
    dot = functools.partial(jnp.dot, preferred_element_type=F32)
    return dot(a_hi, b_hi) + (dot(a_hi, b_lo) + dot(a_lo, b_hi))


def _split3(b):
    b_hi, b_lo = _split(b)
    return b_hi, b_lo, (b - b_hi.astype(F32) - b_lo.astype(F32)).astype(BF16)


def _mm_cumsum(tri01, b):
    b_hi, b_lo, b_lo2 = _split3(b)
    t = tri01.astype(BF16)
    dot = functools.partial(jnp.dot, preferred_element_type=F32)
    return dot(t, b_hi) + (dot(t, b_lo) + dot(t, b_lo2))


def _mm_cumsum_rhs(a, tri01):
    a_hi, a_lo, a_lo2 = _split3(a)
    t = tri01.astype(BF16)
    dot = functools.partial(jnp.dot, preferred_element_type=F32)
    return dot(a_hi, t) + (dot(a_lo, t) + dot(a_lo2, t))


def _sigmoid(x):
    return 1.0 / (1.0 + jnp.exp(-x))


def _softplus(x):
    return jnp.maximum(x, 0.0) + jnp.log(1.0 + jnp.exp(-jnp.abs(x)))


def _rms(x, g):
    return x * lax.rsqrt(jnp.mean(x * x, axis=-1, keepdims=True) + EPS) * g


def _tri(n, strict=False):
    r = lax.broadcasted_iota(jnp.int32, (n, n), 0)
    c = lax.broadcasted_iota(jnp.int32, (n, n), 1)
    return (r > c) if strict else (r >= c)


def _inproj_kernel(x_ref, g_ref, w_ref, ws_ref, wst_ref, o_ref, os_ref, ost_ref, h_scr):
    @pl.when(pl.program_id(1) == 0)
    def _():
        hb = _rms(x_ref[...], g_ref[...]).astype(BF16)
        h_scr[...] = hb
        os_ref[...] = jnp.dot(hb, ws_ref[...], preferred_element_type=F32)
        ost_ref[...] = lax.dot_general(wst_ref[...], hb, _NT, preferred_element_type=F32)

    o_ref[...] = jnp.dot(h_scr[...], w_ref[...], preferred_element_type=F32)


def _inproj(x2, g, w_main, w_small, w_small_t, tm=1024, tn=1024):
    nt = x2.shape[0]
    ncol = w_main.shape[1]
    return pl.pallas_call(
        _inproj_kernel,
        grid=(nt // tm, ncol // tn),
        in_specs=[
            pl.BlockSpec((tm, D_MODEL), lambda i, j: (i, 0)),
            pl.BlockSpec((1, D_MODEL), lambda i, j: (0, 0)),
            pl.BlockSpec((D_MODEL, tn), lambda i, j: (0, j)),
            pl.BlockSpec((D_MODEL, 128), lambda i, j: (0, 0)),
            pl.BlockSpec((16, D_MODEL), lambda i, j: (0, 0)),
        ],
        out_specs=[
            pl.BlockSpec((tm, tn), lambda i, j: (i, j)),
            pl.BlockSpec((tm, 128), lambda i, j: (i, 0)),
            pl.BlockSpec((16, tm), lambda i, j: (0, i)),
        ],
        out_shape=[
            jax.ShapeDtypeStruct((nt, ncol), F32),
            jax.ShapeDtypeStruct((nt, 128), F32),
            jax.ShapeDtypeStruct((16, nt), F32),
        ],
        scratch_shapes=[pltpu.VMEM((tm, D_MODEL), BF16)],
        compiler_params=pltpu.CompilerParams(
            dimension_semantics=("arbitrary", "arbitrary"), vmem_limit_bytes=VMEM_LIMIT),
        name="inproj",
    )(x2, g, w_main, w_small, w_small_t)


def _gdn_kernel(q_ref, k_ref, v_ref, gate_ref, sm_ref, *rest, tblk, bsz):
    smt_refs = rest[:bsz]
    (conv_ref, alog_r, dtb_r, alog_c, dtb_c, onorm_ref, o_ref,
     s_scr, xpad_scr, qkv_scr, grow_scr) = rest[bsz:]
    nc = tblk // CHUNK

    @pl.when(pl.program_id(0) == 0)
    def _():
        s_scr[...] = jnp.zeros_like(s_scr)
        xpad_scr[:, 0:PAD_ROWS, :] = jnp.zeros((bsz, PAD_ROWS, 3 * D_MODEL), F32)

    r = lax.broadcasted_iota(jnp.int32, (tblk, tblk), 0)
    c = lax.broadcasted_iota(jnp.int32, (tblk, tblk), 1)
    lg_chunk = CHUNK.bit_length() - 1
    same_chunk = jnp.right_shift(r, lg_chunk) == jnp.right_shift(c, lg_chunk)
    ublk = jnp.where(r <= c, jnp.where(same_chunk, 1.0, 0.0), 0.0).astype(F32)
    conv_w = conv_ref[...]
    col_grp = 512
    for b in range(bsz):
        for idx, ref in enumerate((q_ref, k_ref, v_ref)):
            xpad_scr[b, PAD_ROWS:PAD_ROWS + tblk, idx * D_MODEL:(idx + 1) * D_MODEL] = ref[b]
        for ci in range(nc):
            for cg in range(3 * D_MODEL // col_grp):
                cs = slice(cg * col_grp, (cg + 1) * col_grp)
                r0 = PAD_ROWS + ci * CHUNK
                y = xpad_scr[b, r0:r0 + CHUNK, cs] * conv_w[CONV_W - 1:CONV_W, cs]
                for j in range(CONV_W - 1):
                    off = r0 - (CONV_W - 1) + j
                    y = y + xpad_scr[b, off:off + CHUNK, cs] * conv_w[j:j + 1, cs]
                qkv_scr[b, ci * CHUNK:(ci + 1) * CHUNK, cs] = y * _sigmoid(y)
        xpad_scr[b, 0:PAD_ROWS, :] = xpad_scr[b, tblk:tblk + PAD_ROWS, :]

        g_t = -jnp.exp(alog_c[...]) * _softplus(smt_refs[b][...] + dtb_c[...])
        grow = _mm_cumsum_rhs(g_t, ublk)
        for ci in range(nc):
            grow_scr[b, ci] = grow[:, ci * CHUNK:(ci + 1) * CHUNK]

    ltri = jnp.where(_tri(CHUNK), 1.0, 0.0).astype(F32)
    causal = _tri(CHUNK)
    strict = _tri(CHUNK, strict=True)
    onorm = onorm_ref[...]
    alog_row = alog_r[...]
    dtb_row = dtb_r[...]

    def chunk_body(ci, carry):
        rows = pl.ds(pl.multiple_of(ci * CHUNK, CHUNK), CHUNK)
        beta_w, gcol_w, grow_c = [], [], []
        for b in range(bsz):
            sm = sm_ref[b, rows, :]
            beta_w.append(_sigmoid(sm))
            g_w = -jnp.exp(alog_row) * _softplus(sm + dtb_row)
            gcol_w.append(_mm_cumsum(ltri, g_w))
            grow_c.append(grow_scr[b, ci])
        chains = [(b, h) for b in range(bsz) for h in range(HEADS)]
        hs = range(len(chains))
        hcols = [slice(h * HEAD_DIM, (h + 1) * HEAD_DIM) for _, h in chains]

        p, xs, a_qk, q_dec, k_dec, g_last = [], [], [], [], [], []
        for b, h in chains:
            qc = qkv_scr[b, rows, h * HEAD_DIM:(h + 1) * HEAD_DIM]
            kc = qkv_scr[b, rows, D_MODEL + h * HEAD_DIM:D_MODEL + (h + 1) * HEAD_DIM]
            vc = qkv_scr[b, rows, 2 * D_MODEL + h * HEAD_DIM:2 * D_MODEL + (h + 1) * HEAD_DIM]
            q = qc * (lax.rsqrt(jnp.sum(qc * qc, axis=-1, keepdims=True) + EPS) * (HEAD_DIM ** -0.5))
            k = kc * lax.rsqrt(jnp.sum(kc * kc, axis=-1, keepdims=True) + EPS)
            beta = beta_w[b][:, h:h + 1]
            g_c = gcol_w[b][:, HEADS + h:HEADS + h + 1]
            g_r = grow_c[b][HEADS + h:HEADS + h + 1, :]
            gl = g_r[:, CHUNK - 1:CHUNK]
            decay = jnp.where(causal, jnp.exp(jnp.where(causal, g_c - g_r, 0.0)), 0.0)
            eg = jnp.exp(g_c)
            kb = k * beta
            p.append(-jnp.where(strict, _mm_nt(kb, k) * decay, 0.0))
            a_qk.append(jnp.where(causal, _mm_nt(q, k) * decay, 0.0))
            xs.append(jnp.concatenate([vc * beta, kb * eg], axis=-1))
            q_dec.append(q * eg)
            k_dec.append(k * jnp.exp(gl - g_c))
            g_last.append(jnp.exp(gl))

        n_lvl = CHUNK.bit_length() - 1
        for lvl in range(n_lvl):
            last = lvl == n_lvl - 1
            for h in hs:
                rhs = xs[h] if last else jnp.concatenate([xs[h], p[h]], axis=-1)
                r = _mm(p[h], rhs)
                xs[h] = xs[h] + r[:, :2 * HEAD_DIM]
                if not last:
                    p[h] = r[:, 2 * HEAD_DIM:]

        s_old = [s_scr[h] for h in hs]
        v_new = [xs[h][:, :HEAD_DIM] - _mm(xs[h][:, HEAD_DIM:], s_old[h]) for h in hs]
        for ch, (b, h) in enumerate(chains):
            o = _mm(q_dec[ch], s_old[ch]) + _mm(a_qk[ch], v_new[ch])
            s_scr[ch] = s_old[ch] * g_last[ch] + _mm_tn(k_dec[ch], v_new[ch])
            gate = gate_ref[b, rows, hcols[ch]]
            o_ref[b, rows, hcols[ch]] = (
                _rms(o, onorm) * (gate * _sigmoid(gate))).astype(o_ref.dtype)
        return carry

    lax.fori_loop(0, nc, chunk_body, 0)


def _gdn(proj, small, small_t, conv, alog_r, dtb_r, alog_c, dtb_c, onorm, bsz, t, tblk=256):
    nb = t // tblk
    blk = lambda col: pl.BlockSpec((bsz, tblk, D_MODEL), lambda i: (0, i, col))
    full = lambda a: pl.BlockSpec(a.shape, lambda i: (0,) * a.ndim)
    smt_specs = [pl.BlockSpec((16, tblk), functools.partial(lambda i, b: (0, b * nb + i), b=b))
                 for b in range(bsz)]
    return pl.pallas_call(
        functools.partial(_gdn_kernel, tblk=tblk, bsz=bsz),
        grid=(nb,),
        in_specs=[blk(0), blk(1), blk(2), blk(3),
                  pl.BlockSpec((bsz, tblk, 128), lambda i: (0, i, 0))] + smt_specs + [
                  full(conv), full(alog_r), full(dtb_r), full(alog_c), full(dtb_c), full(onorm)],
        out_specs=pl.BlockSpec((bsz, tblk, D_MODEL), lambda i: (0, i, 0)),
        out_shape=jax.ShapeDtypeStruct((bsz, t, D_MODEL), BF16),
        scratch_shapes=[
            pltpu.VMEM((bsz * HEADS, HEAD_DIM, HEAD_DIM), F32),
            pltpu.VMEM((bsz, PAD_ROWS + tblk, 3 * D_MODEL), F32),
            pltpu.VMEM((bsz, tblk, 3 * D_MODEL), F32),
            pltpu.VMEM((bsz, tblk // CHUNK, 16, CHUNK), F32),
        ],
        compiler_params=pltpu.CompilerParams(
            dimension_semantics=("arbitrary",), vmem_limit_bytes=VMEM_LIMIT),
        name="gdn",
    )(proj, proj, proj, proj, small, *([small_t] * bsz), conv, alog_r, dtb_r, alog_c, dtb_c, onorm)


def _hgrn_kernel(f_ref, i_ref, q_ref, gate_ref, lb_ref, onorm_ref, o_ref, st_scr, *, tblk):
    nc = tblk // CHUNK

    @pl.when(pl.program_id(1) == 0)
    def _():
        st_scr[...] = jnp.zeros_like(st_scr)

    lbp = lb_ref[...]
    e = jnp.exp(lbp - jnp.max(lbp, axis=0, keepdims=True))
    lb = e[0:1] / jnp.sum(e, axis=0, keepdims=True)
    ltri = jnp.where(_tri(CHUNK), 1.0, 0.0).astype(F32)
    onorm = onorm_ref[...]

    def chunk_body(ci, carry):
        rows = pl.ds(pl.multiple_of(ci * CHUNK, CHUNK), CHUNK)
        f_all = lb + (1.0 - lb) * _sigmoid(f_ref[rows, :])
        lf_all = jnp.log(f_all)
        b_all = _mm_cumsum(ltri, lf_all)
        hs = range(HEADS)
        hcols = [slice(h * HEAD_DIM, (h + 1) * HEAD_DIM) for h in hs]
        q_all = q_ref[rows, :]
        q_all = q_all * _sigmoid(q_all)
        k_all = 1.0 - f_all
        eb_all = jnp.exp(b_all)
        n_sub = CHUNK // SUB

        a_blk = []
        for h in hs:
            b = b_all[:, hcols[h]]
            lf = lf_all[:, hcols[h]]
            kk = k_all[:, hcols[h]]
            q = q_all[:, hcols[h]]
            blks = []
            for blk in range(n_sub):
                r0 = blk * SUB
                n = r0 + SUB
                ref = b[r0:r0 + 1, :] - lf[r0:r0 + 1, :]
                qt = q[r0:n] * jnp.exp(b[r0:n] - ref)
                kt = kk[:n] * jnp.exp(ref - b[:n])
                a = _mm_nt(qt, kt)
                rr = lax.broadcasted_iota(jnp.int32, (SUB, n), 0)
                cc = lax.broadcasted_iota(jnp.int32, (SUB, n), 1)
                blks.append(jnp.where(cc <= rr + r0, a, 0.0))
            a_blk.append(blks)
        st_old = [st_scr[h] for h in hs]
        outs = []
        for h in hs:
            v = i_ref[rows, hcols[h]]
            intra = [_mm(a_blk[h][blk], v[:(blk + 1) * SUB]) for blk in range(n_sub)]
            inter = _mm_nt(q_all[:, hcols[h]] * eb_all[:, hcols[h]], st_old[h])
            outs.append(jnp.concatenate(intra, axis=0) + inter)
        for h in hs:
            b = b_all[:, hcols[h]]
            b_last = b[CHUNK - 1:CHUNK, :]
            v = i_ref[rows, hcols[h]]
            st_scr[h] = (st_old[h] * jnp.exp(b_last)
                         + _mm_tn(v, k_all[:, hcols[h]] * jnp.exp(b_last - b)))
            gate = gate_ref[rows, hcols[h]]
            o_ref[rows, hcols[h]] = (_rms(outs[h], onorm) * _sigmoid(gate)).astype(o_ref.dtype)
        return carry

    lax.fori_loop(0, nc, chunk_body, 0)


def _hgrn(proj, lbp, onorm, bsz, t, tblk=256):
    nt = bsz * t
    nb = t // tblk
    row = lambda b, i: b * nb + i
    blk = lambda col: pl.BlockSpec((tblk, D_MODEL), lambda b, i: (row(b, i), col))
    full = lambda a: pl.BlockSpec(a.shape, lambda b, i: (0,) * a.ndim)
    return pl.pallas_call(
        functools.partial(_hgrn_kernel, tblk=tblk),
        grid=(bsz, nb),
        in_specs=[blk(4), blk(5), blk(6), blk(7), full(lbp), full(onorm)],
        out_specs=pl.BlockSpec((tblk, D_MODEL), lambda b, i: (row(b, i), 0)),
        out_shape=jax.ShapeDtypeStruct((nt, D_MODEL), BF16),
        scratch_shapes=[pltpu.VMEM((HEADS, HEAD_DIM, HEAD_DIM), F32)],
        compiler_params=pltpu.CompilerParams(
            dimension_semantics=("arbitrary", "arbitrary"), vmem_limit_bytes=VMEM_LIMIT),
        name="hgrn2",
    )(proj, proj, proj, proj, lbp, onorm)


def _cand_pairs():
    return [(a, b) for a in range(P_TOPK) for b in range(P_TOPK) if (a + 1) * (b + 1) <= P_TOPK]


def _extract_top(s, n, with_rank=False):
    tops = []
    cur = s
    rank = jnp.full(s.shape, RANK_NONE, F32) if with_rank else None
    for r in range(n):
        m = jnp.max(cur, axis=0, keepdims=True)
        tops.append(m)
        hit = cur == m
        if with_rank:
            rank = jnp.where(hit, float(r), rank)
        cur = jnp.where(hit, -jnp.inf, cur)
    return (tops, rank) if with_rank else tops


def _merge_kernel(oa_ref, ob_ref, ga_ref, gb_ref, x_ref, wa_ref, wb_ref, wo_ref, n2_ref,
                  wpq_ref, keys_ref, x1_ref, ht_ref, e1_ref, cnt_ref, e2_ref, rk_ref):
    ya = jnp.dot(oa_ref[...], wa_ref[...], preferred_element_type=F32)
    yb = jnp.dot(ob_ref[...], wb_ref[...], preferred_element_type=F32)
    mix = _sigmoid(ga_ref[...]) * ya + _sigmoid(gb_ref[...]) * yb
    x1 = x_ref[...] + _mm(mix, wo_ref[...])
    x1_ref[...] = x1
    h2 = _rms(x1, n2_ref[...])
    ht_ref[...] = h2.T.astype(BF16)
    qp = _mm(h2, wpq_ref[...])
    pairs = _cand_pairs()
    for h in range(P_HEADS):
        c0 = h * 2 * HEAD_DIM
        s1 = _mm_nt(keys_ref[0, h], qp[:, c0:c0 + HEAD_DIM])
        s2 = _mm_nt(keys_ref[1, h], qp[:, c0 + HEAD_DIM:c0 + 2 * HEAD_DIM])
        top1 = _extract_top(s1, P_TOPK)
        top2, rank2 = _extract_top(s2, P_TOPK, with_rank=True)
        cand = jnp.concatenate([top1[a] + top2[b] for a, b in pairs], axis=0)
        ctop = _extract_top(cand, P_TOPK)
        c_max = ctop[0]
        tau = ctop[P_TOPK - 1]
        z = jnp.exp(ctop[0] - c_max)
        for r_ in range(1, P_TOPK):
            z = z + jnp.exp(ctop[r_] - c_max)
        inv_z = 1.0 / z
        in1 = s1 >= top1[P_TOPK - 1]
        cnt = jnp.zeros_like(s1)
        for r2 in range(P_TOPK):
            cnt = cnt + jnp.where(s1 + top2[r2] >= tau, 1.0, 0.0)
        e1_ref[h] = jnp.where(in1, jnp.exp(s1 - top1[0]), 0.0)
        cnt_ref[h] = jnp.where(in1, cnt, 0.0)
        e2_ref[h] = (jnp.where(rank2 < RANK_NONE, jnp.exp(s2 - top2[0]), 0.0) * inv_z).astype(BF16)
        rk_ref[h] = rank2.astype(BF16)


def _merge(oa, ob, proj, x2, wa, wb, wo, n2, wpq, keys, tb=256):
    nt = x2.shape[0]
    rowblk = lambda col: pl.BlockSpec((tb, D_MODEL), lambda i: (i, col))
    full = lambda a: pl.BlockSpec(a.shape, lambda i: (0,) * a.ndim)
    return pl.pallas_call(
        _merge_kernel,
        grid=(nt // tb,),
        in_specs=[rowblk(0), rowblk(0), rowblk(8), rowblk(9), rowblk(0),
                  full(wa), full(wb), full(wo), full(n2), full(wpq), full(keys)],
        out_specs=[
            pl.BlockSpec((tb, D_MODEL), lambda i: (i, 0)),
            pl.BlockSpec((D_MODEL, tb), lambda i: (0, i)),
            pl.BlockSpec((P_HEADS, N_KEYS, tb), lambda i: (0, 0, i)),
            pl.BlockSpec((P_HEADS, N_KEYS, tb), lambda i: (0, 0, i)),
            pl.BlockSpec((P_HEADS, N_KEYS, tb), lambda i: (0, 0, i)),
            pl.BlockSpec((P_HEADS, N_KEYS, tb), lambda i: (0, 0, i)),
        ],
        out_shape=[
            jax.ShapeDtypeStruct((nt, D_MODEL), F32),
            jax.ShapeDtypeStruct((D_MODEL, nt), BF16),
            jax.ShapeDtypeStruct((P_HEADS, N_KEYS, nt), F32),
            jax.ShapeDtypeStruct((P_HEADS, N_KEYS, nt), F32),
            jax.ShapeDtypeStruct((P_HEADS, N_KEYS, nt), BF16),
            jax.ShapeDtypeStruct((P_HEADS, N_KEYS, nt), BF16),
        ],
        compiler_params=pltpu.CompilerParams(
            dimension_semantics=("arbitrary",), vmem_limit_bytes=VMEM_LIMIT),
        name="merge",
    )(oa, ob, proj, proj, x2, wa, wb, wo, n2, wpq, keys)


def _peer_kernel(ht_ref, e1_ref, cnt_ref, e2_ref, rk_ref, u_ref, vt_ref, x1_ref, fn_ref, o_ref,
                 acc_scr, *, eb):
    j = pl.program_id(1)
    tb = ht_ref.shape[1]

    @pl.when(j == 0)
    def _():
        acc_scr[...] = jnp.zeros_like(acc_scr)

    ht = ht_ref[...]
    per = MXU_DEPTH // N_KEYS
    n_grp = eb // MXU_DEPTH
    grp_rows = [slice(g * MXU_DEPTH, (g + 1) * MXU_DEPTH) for g in range(n_grp)]
    hid_next = jnp.dot(u_ref[grp_rows[0], :], ht, preferred_element_type=F32)
    for g in range(n_grp):
        rows = grp_rows[g]
        hid = hid_next
        if g + 1 < n_grp:
            hid_next = jnp.dot(u_ref[grp_rows[g + 1], :], ht, preferred_element_type=F32)
        parts = []
        for sub in range(per):
            i1 = j * (eb // N_KEYS) + g * per + sub
            wgt = jnp.zeros((N_KEYS, tb), BF16)
            for h in range(P_HEADS):
                e1 = jnp.broadcast_to(e1_ref[h, pl.ds(i1, 1), :], (N_KEYS, tb)).astype(BF16)
                cnt = jnp.broadcast_to(cnt_ref[h, pl.ds(i1, 1), :], (N_KEYS, tb)).astype(BF16)
                wgt = wgt + jnp.where(rk_ref[h] < cnt, e2_ref[h] * e1, jnp.zeros((), BF16))
            parts.append(wgt)
        wgt = jnp.concatenate(parts, axis=0)
        hb = hid.astype(BF16)
        half = hb * jnp.asarray(0.5, BF16)
        act = half * lax.erf(hb * jnp.asarray(2.0 ** -0.5, BF16)) + half
        acc_scr[...] += jnp.dot(vt_ref[:, rows], wgt * act, preferred_element_type=F32)

    @pl.when(j == pl.num_programs(1) - 1)
    def _():
        o_ref[...] = _rms(x1_ref[...] + acc_scr[...].T, fn_ref[...])


def _peer(ht, e1, cnt, e2, rk, u, vt, x1, fn, tb=512, eb=2048):
    nt = x1.shape[0]
    ne = u.shape[0]
    return pl.pallas_call(
        functools.partial(_peer_kernel, eb=eb),
        grid=(nt // tb, ne // eb),
        in_specs=[
            pl.BlockSpec((D_MODEL, tb), lambda i, j: (0, i)),
            pl.BlockSpec((P_HEADS, N_KEYS, tb), lambda i, j: (0, 0, i)),
            pl.BlockSpec((P_HEADS, N_KEYS, tb), lambda i, j: (0, 0, i)),
            pl.BlockSpec((P_HEADS, N_KEYS, tb), lambda i, j: (0, 0, i)),
            pl.BlockSpec((P_HEADS, N_KEYS, tb), lambda i, j: (0, 0, i)),
            pl.BlockSpec((eb, D_MODEL), lambda i, j: (j, 0)),
            pl.BlockSpec((D_MODEL, eb), lambda i, j: (0, j)),
            pl.BlockSpec((tb, D_MODEL), lambda i, j: (i, 0)),
            pl.BlockSpec((1, D_MODEL), lambda i, j: (0, 0)),
        ],
        out_specs=pl.BlockSpec((tb, D_MODEL), lambda i, j: (i, 0)),
        out_shape=jax.ShapeDtypeStruct((nt, D_MODEL), F32),
        scratch_shapes=[pltpu.VMEM((D_MODEL, tb), F32)],
        compiler_params=pltpu.CompilerParams(
            dimension_semantics=("arbitrary", "arbitrary"), vmem_limit_bytes=VMEM_LIMIT),
        name="peer",
    )(ht, e1, cnt, e2, rk, u, vt, x1, fn)


def _pad_lanes(row, offset):
    return jnp.zeros((1, 128), F32).at[0, offset:offset + row.shape[0]].set(row.astype(F32))


def kernel(x, norm1, w_in, conv_a, a_log, dt_bias, a_onorm, b_lower_bound, b_onorm, w_branch_a,
           w_branch_b, w_out, norm2, w_pq, sub_keys, expert_u, expert_v, final_norm):
    assert norm1.shape[0] == 1, "one layer"
    bsz, t, _ = x.shape
    nt = bsz * t
    x2 = x.reshape(nt, D_MODEL)

    wi = w_in[0]
    n_qkv = 3 * D_MODEL
    n_used = 10 * D_MODEL + 2 * HEADS
    w_main = jnp.concatenate([wi[:, :n_qkv], wi[:, n_qkv + 2 * HEADS:n_used]], axis=1).astype(BF16)
    w_small = jnp.pad(wi[:, n_qkv:n_qkv + 2 * HEADS], ((0, 0), (0, 128 - 2 * HEADS))).astype(BF16)
    w_small_t = wi[:, n_qkv:n_qkv + 2 * HEADS].T.astype(BF16)

    proj, small, small_t = _inproj(x2, norm1[0][None, :], w_main, w_small, w_small_t)

    alog_r = _pad_lanes(a_log[0], HEADS)
    dtb_r = _pad_lanes(dt_bias[0], HEADS)
    alog_c = alog_r[0, :16][:, None]
    dtb_c = dtb_r[0, :16][:, None]
    oa = _gdn(proj.reshape(bsz, t, -1), small.reshape(bsz, t, -1), small_t, conv_a[0], alog_r,
              dtb_r, alog_c, dtb_c, a_onorm[0][None, :], bsz, t).reshape(nt, D_MODEL)
    ob = _hgrn(proj, b_lower_bound, b_onorm[0][None, :], bsz, t)

    x1, ht, e1, cnt, e2, rk = _merge(
        oa, ob, proj, x2, w_branch_a[0].astype(BF16), w_branch_b[0].astype(BF16),
        w_out[0].astype(BF16), norm2[0][None, :], w_pq[0].astype(BF16), sub_keys[0].astype(BF16))

    out = _peer(ht, e1, cnt, e2, rk, expert_u[0].astype(BF16), expert_v[0].T.astype(BF16), x1,
                final_norm[None, :])
    return out.reshape(bsz, t, D_MODEL)
```

```python
import functools

import jax
import jax.numpy as jnp
from jax import lax
from jax.experimental import pallas as pl
from jax.experimental.pallas import tpu as pltpu

F32 = jnp.float32
BF16 = jnp.bfloat16
EPS = 1e-6

D_MODEL = 1024
HEADS = 8
HEAD_DIM = 128
CHUNK = 64
CONV_W = 4
PAD_ROWS = 8
SUB = 16

N_KEYS = 128
P_HEADS = 8
P_TOPK = 16
RANK_NONE = 64.0
MXU_DEPTH = 256

VMEM_LIMIT = 56 * 1024 * 1024

_NT = (((1,), (1,)), ((), ()))
_TN = (((0,), (0,)), ((), ()))


def _mm(a, b):
    return jnp.dot(a.astype(BF16), b.astype(BF16), preferred_element_type=F32)


def _mm_nt(a, b):
    return lax.dot_general(a.astype(BF16), b.astype(BF16), _NT, preferred_element_type=F32)


def _mm_tn(a, b):
    return lax.dot_general(a.astype(BF16), b.astype(BF16), _TN, preferred_element_type=F32)


def _split(a):
    hi = a.astype(BF16)
    return hi, (a - hi.astype(F32)).astype(BF16)


def _mm_solve(a, b):
    a_hi, a_lo = _split(a)
    b_hi, b_lo = _split(b)
    dot = functools.partial(jnp.dot, preferred_element_type=F32)
    return dot(a_hi, b_hi) + (dot(a_hi, b_lo) + dot(a_lo, b_hi))


def _split3(b):
    b_hi, b_lo = _split(b)
    return b_hi, b_lo, (b - b_hi.astype(F32) - b_lo.astype(F32)).astype(BF16)


def _mm_cumsum(tri01, b):
    b_hi, b_lo, b_lo2 = _split3(b)
    t = tri01.astype(BF16)
    dot = functools.partial(jnp.dot, preferred_element_type=F32)
    return dot(t, b_hi) + (dot(t, b_lo) + dot(t, b_lo2))


def _mm_cumsum_rhs(a, tri01):
    a_hi, a_lo, a_lo2 = _split3(a)
    t = tri01.astype(BF16)
    dot = functools.partial(jnp.dot, preferred_element_type=F32)
    return dot(a_hi, t) + (dot(a_lo, t) + dot(a_lo2, t))


def _sigmoid(x):
    return 1.0 / (1.0 + jnp.exp(-x))


def _softplus(x):
    return jnp.maximum(x, 0.0) + jnp.log(1.0 + jnp.exp(-jnp.abs(x)))


def _rms(x, g):
    return x * lax.rsqrt(jnp.mean(x * x, axis=-1, keepdims=True) + EPS) * g


def _tri(n, strict=False):
    r = lax.broadcasted_iota(jnp.int32, (n, n), 0)
    c = lax.broadcasted_iota(jnp.int32, (n, n), 1)
    return (r > c) if strict else (r >= c)


def _inproj_kernel(x_ref, g_ref, w_ref, ws_ref, wst_ref, o_ref, os_ref, ost_ref, h_scr):
    @pl.when(pl.program_id(1) == 0)
    def _():
        hb = _rms(x_ref[...], g_ref[...]).astype(BF16)
        h_scr[...] = hb
        os_ref[...] = jnp.dot(hb, ws_ref[...], preferred_element_type=F32)
        ost_ref[...] = lax.dot_general(wst_ref[...], hb, _NT, preferred_element_type=F32)

    o_ref[...] = jnp.dot(h_scr[...], w_ref[...], preferred_element_type=F32)


def _inproj(x2, g, w_main, w_small, w_small_t, tm=1024, tn=1024):
    nt = x2.shape[0]
    ncol = w_main.shape[1]
    return pl.pallas_call(
        _inproj_kernel,
        grid=(nt // tm, ncol // tn),
        in_specs=[
            pl.BlockSpec((tm, D_MODEL), lambda i, j: (i, 0)),
            pl.BlockSpec((1, D_MODEL), lambda i, j: (0, 0)),
            pl.BlockSpec((D_MODEL, tn), lambda i, j: (0, j)),
            pl.BlockSpec((D_MODEL, 128), lambda i, j: (0, 0)),
            pl.BlockSpec((16, D_MODEL), lambda i, j: (0, 0)),
        ],
        out_specs=[
            pl.BlockSpec((tm, tn), lambda i, j: (i, j)),
            pl.BlockSpec((tm, 128), lambda i, j: (i, 0)),
            pl.BlockSpec((16, tm), lambda i, j: (0, i)),
        ],
        out_shape=[
            jax.ShapeDtypeStruct((nt, ncol), F32),
            jax.ShapeDtypeStruct((nt, 128), F32),
            jax.ShapeDtypeStruct((16, nt), F32),
        ],
        scratch_shapes=[pltpu.VMEM((tm, D_MODEL), BF16)],
        compiler_params=pltpu.CompilerParams(
            dimension_semantics=("arbitrary", "arbitrary"), vmem_limit_bytes=VMEM_LIMIT),
        name="inproj",
    )(x2, g, w_main, w_small, w_small_t)


def _gdn_kernel(q_ref, k_ref, v_ref, gate_ref, sm_ref, *rest, tblk, bsz):
    smt_refs = rest[:bsz]
    (conv_ref, alog_r, dtb_r, alog_c, dtb_c, onorm_ref, o_ref,
     s_scr, xpad_scr, qkv_scr, grow_scr) = rest[bsz:]
    nc = tblk // CHUNK

    @pl.when(pl.program_id(0) == 0)
    def _():
        s_scr[...] = jnp.zeros_like(s_scr)
        xpad_scr[:, 0:PAD_ROWS, :] = jnp.zeros((bsz, PAD_ROWS, 3 * D_MODEL), F32)

    r = lax.broadcasted_iota(jnp.int32, (tblk, tblk), 0)
    c = lax.broadcasted_iota(jnp.int32, (tblk, tblk), 1)
    lg_chunk = CHUNK.bit_length() - 1
    same_chunk = jnp.right_shift(r, lg_chunk) == jnp.right_shift(c, lg_chunk)
    ublk = jnp.where(r <= c, jnp.where(same_chunk, 1.0, 0.0), 0.0).astype(F32)
    conv_w = conv_ref[...]
    col_grp = 512
    for b in range(bsz):
        for idx, ref in enumerate((q_ref, k_ref, v_ref)):
            xpad_scr[b, PAD_ROWS:PAD_ROWS + tblk, idx * D_MODEL:(idx + 1) * D_MODEL] = ref[b]
        for ci in range(nc):
            for cg in range(3 * D_MODEL // col_grp):
                cs = slice(cg * col_grp, (cg + 1) * col_grp)
                r0 = PAD_ROWS + ci * CHUNK
                y = xpad_scr[b, r0:r0 + CHUNK, cs] * conv_w[CONV_W - 1:CONV_W, cs]
                for j in range(CONV_W - 1):
                    off = r0 - (CONV_W - 1) + j
                    y = y + xpad_scr[b, off:off + CHUNK, cs] * conv_w[j:j + 1, cs]
                qkv_scr[b, ci * CHUNK:(ci + 1) * CHUNK, cs] = y * _sigmoid(y)
        xpad_scr[b, 0:PAD_ROWS, :] = xpad_scr[b, tblk:tblk + PAD_ROWS, :]

        g_t = -jnp.exp(alog_c[...]) * _softplus(smt_refs[b][...] + dtb_c[...])
        grow = _mm_cumsum_rhs(g_t, ublk)
        for ci in range(nc):
            grow_scr[b, ci] = grow[:, ci * CHUNK:(ci + 1) * CHUNK]

    ltri = jnp.where(_tri(CHUNK), 1.0, 0.0).astype(F32)
    causal = _tri(CHUNK)
    strict = _tri(CHUNK, strict=True)
    onorm = onorm_ref[...]
    alog_row = alog_r[...]
    dtb_row = dtb_r[...]

    def chunk_body(ci, carry):
        rows = pl.ds(pl.multiple_of(ci * CHUNK, CHUNK), CHUNK)
        beta_w, gcol_w, grow_c = [], [], []
        for b in range(bsz):
            sm = sm_ref[b, rows, :]
            beta_w.append(_sigmoid(sm))
            g_w = -jnp.exp(alog_row) * _softplus(sm + dtb_row)
            gcol_w.append(_mm_cumsum(ltri, g_w))
            grow_c.append(grow_scr[b, ci])
        chains = [(b, h) for b in range(bsz) for h in range(HEADS)]
        hs = range(len(chains))
        hcols = [slice(h * HEAD_DIM, (h + 1) * HEAD_DIM) for _, h in chains]

        p, xs, a_qk, q_dec, k_dec, g_last = [], [], [], [], [], []
        for b, h in chains:
            qc = qkv_scr[b, rows, h * HEAD_DIM:(h + 1) * HEAD_DIM]
            kc = qkv_scr[b, rows, D_MODEL + h * HEAD_DIM:D_MODEL + (h + 1) * HEAD_DIM]
            vc = qkv_scr[b, rows, 2 * D_MODEL + h * HEAD_DIM:2 * D_MODEL + (h + 1) * HEAD_DIM]
            q = qc * (lax.rsqrt(jnp.sum(qc * qc, axis=-1, keepdims=True) + EPS) * (HEAD_DIM ** -0.5))
            k = kc * lax.rsqrt(jnp.sum(kc * kc, axis=-1, keepdims=True) + EPS)
            beta = beta_w[b][:, h:h + 1]
            g_c = gcol_w[b][:, HEADS + h:HEADS + h + 1]
            g_r = grow_c[b][HEADS + h:HEADS + h + 1, :]
            gl = g_r[:, CHUNK - 1:CHUNK]
            decay = jnp.where(causal, jnp.exp(jnp.where(causal, g_c - g_r, 0.0)), 0.0)
            eg = jnp.exp(g_c)
            kb = k * beta
            p.append(-jnp.where(strict, _mm_nt(kb, k) * decay, 0.0))
            a_qk.append(jnp.where(causal, _mm_nt(q, k) * decay, 0.0))
            xs.append(jnp.concatenate([vc * beta, kb * eg], axis=-1))
            q_dec.append(q * eg)
            k_dec.append(k * jnp.exp(gl - g_c))
            g_last.append(jnp.exp(gl))

        n_lvl = CHUNK.bit_length() - 1
        for lvl in range(n_lvl):
            last = lvl == n_lvl - 1
            for h in hs:
                rhs = xs[h] if last else jnp.concatenate([xs[h], p[h]], axis=-1)
                r = _mm(p[h], rhs)
                xs[h] = xs[h] + r[:, :2 * HEAD_DIM]
                if not last:
                    p[h] = r[:, 2 * HEAD_DIM:]

        s_old = [s_scr[h] for h in hs]
        v_new = [xs[h][:, :HEAD_DIM] - _mm(xs[h][:, HEAD_DIM:], s_old[h]) for h in hs]
        for ch, (b, h) in enumerate(chains):
            o = _mm(q_dec[ch], s_old[ch]) + _mm(a_qk[ch], v_new[ch])
            s_scr[ch] = s_old[ch] * g_last[ch] + _mm_tn(k_dec[ch], v_new[ch])
            gate = gate_ref[b, rows, hcols[ch]]
            o_ref[b, rows, hcols[ch]] = (
                _rms(o, onorm) * (gate * _sigmoid(gate))).astype(o_ref.dtype)
        return carry

    lax.fori_loop(0, nc, chunk_body, 0)


def _gdn(proj, small, small_t, conv, alog_r, dtb_r, alog_c, dtb_c, onorm, bsz, t, tblk=256):
    nb = t // tblk
    blk = lambda col: pl.BlockSpec((bsz, tblk, D_MODEL), lambda i: (0, i, col))
    full = lambda a: pl.BlockSpec(a.shape, lambda i: (0,) * a.ndim)
    smt_specs = [pl.BlockSpec((16, tblk), functools.partial(lambda i, b: (0, b * nb + i), b=b))
                 for b in range(bsz)]
    return pl.pallas_call(
        functools.partial(_gdn_kernel, tblk=tblk, bsz=bsz),
        grid=(nb,),
        in_specs=[blk(0), blk(1), blk(2), blk(3),
                  pl.BlockSpec((bsz, tblk, 128), lambda i: (0, i, 0))] + smt_specs + [
                  full(conv), full(alog_r), full(dtb_r), full(alog_c), full(dtb_c), full(onorm)],
        out_specs=pl.BlockSpec((bsz, tblk, D_MODEL), lambda i: (0, i, 0)),
        out_shape=jax.ShapeDtypeStruct((bsz, t, D_MODEL), BF16),
        scratch_shapes=[
            pltpu.VMEM((bsz * HEADS, HEAD_DIM, HEAD_DIM), F32),
            pltpu.VMEM((bsz, PAD_ROWS + tblk, 3 * D_MODEL), F32),
            pltpu.VMEM((bsz, tblk, 3 * D_MODEL), F32),
            pltpu.VMEM((bsz, tblk // CHUNK, 16, CHUNK), F32),
        ],
        compiler_params=pltpu.CompilerParams(
            dimension_semantics=("arbitrary",), vmem_limit_bytes=VMEM_LIMIT),
        name="gdn",
    )(proj, proj, proj, proj, small, *([small_t] * bsz), conv, alog_r, dtb_r, alog_c, dtb_c, onorm)


def _hgrn_kernel(f_ref, i_ref, q_ref, gate_ref, lb_ref, onorm_ref, o_ref, st_scr, *, tblk, bsz):
    nc = tblk // CHUNK

    @pl.when(pl.program_id(0) == 0)
    def _():
        st_scr[...] = jnp.zeros_like(st_scr)

    lbp = lb_ref[...]
    e = jnp.exp(lbp - jnp.max(lbp, axis=0, keepdims=True))
    lb = e[0:1] / jnp.sum(e, axis=0, keepdims=True)
    ltri = jnp.where(_tri(CHUNK), 1.0, 0.0).astype(F32)
    onorm = onorm_ref[...]

    def chunk_body(ci, carry):
        rows = pl.ds(pl.multiple_of(ci * CHUNK, CHUNK), CHUNK)
        lf_b, b_b, q_b, k_b, eb_b = [], [], [], [], []
        for bi in range(bsz):
            f_all = lb + (1.0 - lb) * _sigmoid(f_ref[bi, rows, :])
            lf_all = jnp.log(f_all)
            b_all = _mm_cumsum(ltri, lf_all)
            q_all = q_ref[bi, rows, :]
            lf_b.append(lf_all)
            b_b.append(b_all)
            q_b.append(q_all * _sigmoid(q_all))
            k_b.append(1.0 - f_all)
            eb_b.append(jnp.exp(b_all))
        chains = [(bi, h) for bi in range(bsz) for h in range(HEADS)]
        hs = range(len(chains))
        hcols = [slice(h * HEAD_DIM, (h + 1) * HEAD_DIM) for _, h in chains]
        n_sub = CHUNK // SUB

        a_blk = []
        for ch, (bi, _) in enumerate(chains):
            b = b_b[bi][:, hcols[ch]]
            lf = lf_b[bi][:, hcols[ch]]
            kk = k_b[bi][:, hcols[ch]]
            q = q_b[bi][:, hcols[ch]]
            blks = []
            for blk in range(n_sub):
                r0 = blk * SUB
                n = r0 + SUB
                ref = b[r0:r0 + 1, :] - lf[r0:r0 + 1, :]
                qt = q[r0:n] * jnp.exp(b[r0:n] - ref)
                kt = kk[:n] * jnp.exp(ref - b[:n])
                a = _mm_nt(qt, kt)
                rr = lax.broadcasted_iota(jnp.int32, (SUB, n), 0)
                cc = lax.broadcasted_iota(jnp.int32, (SUB, n), 1)
                blks.append(jnp.where(cc <= rr + r0, a, 0.0))
            a_blk.append(blks)
        st_old = [st_scr[ch] for ch in hs]
        outs = []
        for ch, (bi, _) in enumerate(chains):
            v = i_ref[bi, rows, hcols[ch]]
            intra = [_mm(a_blk[ch][blk], v[:(blk + 1) * SUB]) for blk in range(n_sub)]
            inter = _mm_nt(q_b[bi][:, hcols[ch]] * eb_b[bi][:, hcols[ch]], st_old[ch])
            outs.append(jnp.concatenate(intra, axis=0) + inter)
        for ch, (bi, _) in enumerate(chains):
            b = b_b[bi][:, hcols[ch]]
            b_last = b[CHUNK - 1:CHUNK, :]
            v = i_ref[bi, rows, hcols[ch]]
            st_scr[ch] = (st_old[ch] * jnp.exp(b_last)
                          + _mm_tn(v, k_b[bi][:, hcols[ch]] * jnp.exp(b_last - b)))
            gate = gate_ref[bi, rows, hcols[ch]]
            o_ref[bi, rows, hcols[ch]] = (
                _rms(outs[ch], onorm) * _sigmoid(gate)).astype(o_ref.dtype)
        return carry

    lax.fori_loop(0, nc, chunk_body, 0)


def _hgrn(proj, lbp, onorm, bsz, t, tblk=256):
    nb = t // tblk
    blk = lambda col: pl.BlockSpec((bsz, tblk, D_MODEL), lambda i: (0, i, col))
    full = lambda a: pl.BlockSpec(a.shape, lambda i: (0,) * a.ndim)
    return pl.pallas_call(
        functools.partial(_hgrn_kernel, tblk=tblk, bsz=bsz),
        grid=(nb,),
        in_specs=[blk(4), blk(5), blk(6), blk(7), full(lbp), full(onorm)],
        out_specs=pl.BlockSpec((bsz, tblk, D_MODEL), lambda i: (0, i, 0)),
        out_shape=jax.ShapeDtypeStruct((bsz, t, D_MODEL), BF16),
        scratch_shapes=[pltpu.VMEM((bsz * HEADS, HEAD_DIM, HEAD_DIM), F32)],
        compiler_params=pltpu.CompilerParams(
            dimension_semantics=("arbitrary",), vmem_limit_bytes=VMEM_LIMIT),
        name="hgrn2",
    )(proj, proj, proj, proj, lbp, onorm)


def _cand_pairs():
    return [(a, b) for a in range(P_TOPK) for b in range(P_TOPK) if (a + 1) * (b + 1) <= P_TOPK]


def _extract_top(s, n, with_rank=False):
    tops = []
    cur = s
    rank = jnp.full(s.shape, RANK_NONE, F32) if with_rank else None
    for r in range(n):
        m = jnp.max(cur, axis=0, keepdims=True)
        tops.append(m)
        hit = cur == m
        if with_rank:
            rank = jnp.where(hit, float(r), rank)
        cur = jnp.where(hit, -jnp.inf, cur)
    return (tops, rank) if with_rank else tops


def _count_kept(s1, top2, tau):
    n = len(top2)
    assert n & (n - 1) == 0

    def row(masks, lo, step):
        if not masks:
            return top2[lo + step - 1]
        weight = step << len(masks)
        return jnp.where(masks[0], row(masks[1:], lo + weight, step), row(masks[1:], lo, step))

    masks, weights = [], []
    step = n // 2
    while step >= 1:
        masks.append(s1 + row(masks, 0, step) >= tau)
        weights.append(float(step))
        step //= 2
    masks.append(s1 + top2[n - 1] >= tau)
    weights.append(1.0)
    cnt = jnp.where(masks[0], weights[0], 0.0)
    for m, w in zip(masks[1:], weights[1:]):
        cnt = cnt + jnp.where(m, w, 0.0)
    return cnt


def _merge_kernel(oa_ref, ob_ref, ga_ref, gb_ref, x_ref, wa_ref, wb_ref, wo_ref, n2_ref,
                  wpq_ref, keys_ref, x1_ref, ht_ref, e1_ref, cnt_ref, e2_ref, rk_ref):
    ya = jnp.dot(oa_ref[...], wa_ref[...], preferred_element_type=F32)
    yb = jnp.dot(ob_ref[...], wb_ref[...], preferred_element_type=F32)
    mix = _sigmoid(ga_ref[...]) * ya + _sigmoid(gb_ref[...]) * yb
    x1 = x_ref[...] + _mm(mix, wo_ref[...])
    x1_ref[...] = x1
    h2 = _rms(x1, n2_ref[...])
    ht_ref[...] = h2.T.astype(BF16)
    qp = _mm(h2, wpq_ref[...])
    pairs = _cand_pairs()
    for h in range(P_HEADS):
        c0 = h * 2 * HEAD_DIM
        s1 = _mm_nt(keys_ref[0, h], qp[:, c0:c0 + HEAD_DIM])
        s2 = _mm_nt(keys_ref[1, h], qp[:, c0 + HEAD_DIM:c0 + 2 * HEAD_DIM])
        top1 = _extract_top(s1, P_TOPK)
        top2, rank2 = _extract_top(s2, P_TOPK, with_rank=True)
        cand = jnp.concatenate([top1[a] + top2[b] for a, b in pairs], axis=0)
        ctop = _extract_top(cand, P_TOPK)
        c_max = ctop[0]
        tau = ctop[P_TOPK - 1]
        z = jnp.exp(ctop[0] - c_max)
        for r_ in range(1, P_TOPK):
            z = z + jnp.exp(ctop[r_] - c_max)
        inv_z = 1.0 / z
        in1 = s1 >= top1[P_TOPK - 1]
        cnt = _count_kept(s1, top2, tau)
        e1_ref[h] = jnp.where(in1, jnp.exp(s1 - top1[0]), 0.0)
        cnt_ref[h] = jnp.where(in1, cnt, 0.0)
        e2_ref[h] = (jnp.where(rank2 < RANK_NONE, jnp.exp(s2 - top2[0]), 0.0) * inv_z).astype(BF16)
        rk_ref[h] = rank2.astype(BF16)


def _merge(oa, ob, proj, x2, wa, wb, wo, n2, wpq, keys, tb=256):
    nt = x2.shape[0]
    rowblk = lambda col: pl.BlockSpec((tb, D_MODEL), lambda i: (i, col))
    full = lambda a: pl.BlockSpec(a.shape, lambda i: (0,) * a.ndim)
    return pl.pallas_call(
        _merge_kernel,
        grid=(nt // tb,),
        in_specs=[rowblk(0), rowblk(0), rowblk(8), rowblk(9), rowblk(0),
                  full(wa), full(wb), full(wo), full(n2), full(wpq), full(keys)],
        out_specs=[
            pl.BlockSpec((tb, D_MODEL), lambda i: (i, 0)),
            pl.BlockSpec((D_MODEL, tb), lambda i: (0, i)),
            pl.BlockSpec((P_HEADS, N_KEYS, tb), lambda i: (0, 0, i)),
            pl.BlockSpec((P_HEADS, N_KEYS, tb), lambda i: (0, 0, i)),
            pl.BlockSpec((P_HEADS, N_KEYS, tb), lambda i: (0, 0, i)),
            pl.BlockSpec((P_HEADS, N_KEYS, tb), lambda i: (0, 0, i)),
        ],
        out_shape=[
            jax.ShapeDtypeStruct((nt, D_MODEL), F32),
            jax.ShapeDtypeStruct((D_MODEL, nt), BF16),
            jax.ShapeDtypeStruct((P_HEADS, N_KEYS, nt), F32),
            jax.ShapeDtypeStruct((P_HEADS, N_KEYS, nt), F32),
            jax.ShapeDtypeStruct((P_HEADS, N_KEYS, nt), BF16),
            jax.ShapeDtypeStruct((P_HEADS, N_KEYS, nt), BF16),
        ],
        compiler_params=pltpu.CompilerParams(
            dimension_semantics=("arbitrary",), vmem_limit_bytes=VMEM_LIMIT),
        name="merge",
    )(oa, ob, proj, proj, x2, wa, wb, wo, n2, wpq, keys)


def _peer_kernel(ht_ref, e1_ref, cnt_ref, e2_ref, rk_ref, u_ref, vt_ref, x1_ref, fn_ref, o_ref,
                 acc_scr, *, eb):
    j = pl.program_id(1)
    tb = ht_ref.shape[1]

    @pl.when(j == 0)
    def _():
        acc_scr[...] = jnp.zeros_like(acc_scr)

    ht = ht_ref[...]
    per = MXU_DEPTH // N_KEYS
    n_grp = eb // MXU_DEPTH
    grp_rows = [slice(g * MXU_DEPTH, (g + 1) * MXU_DEPTH) for g in range(n_grp)]
    hid_next = jnp.dot(u_ref[grp_rows[0], :], ht, preferred_element_type=F32)
    gated_prev = None
    for g in range(n_grp):
        hid = hid_next
        if g + 1 < n_grp:
            hid_next = jnp.dot(u_ref[grp_rows[g + 1], :], ht, preferred_element_type=F32)
        if gated_prev is not None:
            acc_scr[...] += jnp.dot(vt_ref[:, grp_rows[g - 1]], gated_prev,
                                    preferred_element_type=F32)
        parts = []
        for sub in range(per):
            i1 = j * (eb // N_KEYS) + g * per + sub
            wgt = jnp.zeros((N_KEYS, tb), BF16)
            for h in range(P_HEADS):
                e1 = jnp.broadcast_to(e1_ref[h, pl.ds(i1, 1), :], (N_KEYS, tb)).astype(BF16)
                cnt = jnp.broadcast_to(cnt_ref[h, pl.ds(i1, 1), :], (N_KEYS, tb)).astype(BF16)
                wgt = wgt + jnp.where(rk_ref[h] < cnt, e2_ref[h] * e1, jnp.zeros((), BF16))
            parts.append(wgt)
        wgt = jnp.concatenate(parts, axis=0)
        hb = hid.astype(BF16)
        half = hb * jnp.asarray(0.5, BF16)
        act = half * lax.erf(hb * jnp.asarray(2.0 ** -0.5, BF16)) + half
        gated_prev = wgt * act
    acc_scr[...] += jnp.dot(vt_ref[:, grp_rows[n_grp - 1]], gated_prev,
                            preferred_element_type=F32)

    @pl.when(j == pl.num_programs(1) - 1)
    def _():
        o_ref[...] = _rms(x1_ref[...] + acc_scr[...].T, fn_ref[...])


def _peer(ht, e1, cnt, e2, rk, u, vt, x1, fn, tb=512, eb=2048):
    nt = x1.shape[0]
    ne = u.shape[0]
    return pl.pallas_call(
        functools.partial(_peer_kernel, eb=eb),
        grid=(nt // tb, ne // eb),
        in_specs=[
            pl.BlockSpec((D_MODEL, tb), lambda i, j: (0, i)),
            pl.BlockSpec((P_HEADS, N_KEYS, tb), lambda i, j: (0, 0, i)),
            pl.BlockSpec((P_HEADS, N_KEYS, tb), lambda i, j: (0, 0, i)),
            pl.BlockSpec((P_HEADS, N_KEYS, tb), lambda i, j: (0, 0, i)),
            pl.BlockSpec((P_HEADS, N_KEYS, tb), lambda i, j: (0, 0, i)),
            pl.BlockSpec((eb, D_MODEL), lambda i, j: (j, 0)),
            pl.BlockSpec((D_MODEL, eb), lambda i, j: (0, j)),
            pl.BlockSpec((tb, D_MODEL), lambda i, j: (i, 0)),
            pl.BlockSpec((1, D_MODEL), lambda i, j: (0, 0)),
        ],
        out_specs=pl.BlockSpec((tb, D_MODEL), lambda i, j: (i, 0)),
        out_shape=jax.ShapeDtypeStruct((nt, D_MODEL), F32),
        scratch_shapes=[pltpu.VMEM((D_MODEL, tb), F32)],
        compiler_params=pltpu.CompilerParams(
            dimension_semantics=("arbitrary", "arbitrary"), vmem_limit_bytes=VMEM_LIMIT),
        name="peer",
    )(ht, e1, cnt, e2, rk, u, vt, x1, fn)


def _pad_lanes(row, offset):
    return jnp.zeros((1, 128), F32).at[0, offset:offset + row.shape[0]].set(row.astype(F32))


def kernel(x, norm1, w_in, conv_a, a_log, dt_bias, a_onorm, b_lower_bound, b_onorm, w_branch_a,
           w_branch_b, w_out, norm2, w_pq, sub_keys, expert_u, expert_v, final_norm):
    assert norm1.shape[0] == 1, "one layer"
    bsz, t, _ = x.shape
    nt = bsz * t
    x2 = x.reshape(nt, D_MODEL)

    wi = w_in[0]
    n_qkv = 3 * D_MODEL
    n_used = 10 * D_MODEL + 2 * HEADS
    w_main = jnp.concatenate([wi[:, :n_qkv], wi[:, n_qkv + 2 * HEADS:n_used]], axis=1).astype(BF16)
    w_small = jnp.pad(wi[:, n_qkv:n_qkv + 2 * HEADS], ((0, 0), (0, 128 - 2 * HEADS))).astype(BF16)
    w_small_t = wi[:, n_qkv:n_qkv + 2 * HEADS].T.astype(BF16)

    proj, small, small_t = _inproj(x2, norm1[0][None, :], w_main, w_small, w_small_t)

    alog_r = _pad_lanes(a_log[0], HEADS)
    dtb_r = _pad_lanes(dt_bias[0], HEADS)
    alog_c = alog_r[0, :16][:, None]
    dtb_c = dtb_r[0, :16][:, None]
    oa = _gdn(proj.reshape(bsz, t, -1), small.reshape(bsz, t, -1), small_t, conv_a[0], alog_r,
              dtb_r, alog_c, dtb_c, a_onorm[0][None, :], bsz, t).reshape(nt, D_MODEL)
    ob = _hgrn(proj.reshape(bsz, t, -1), b_lower_bound, b_onorm[0][None, :], bsz,
               t).reshape(nt, D_MODEL)

    x1, ht, e1, cnt, e2, rk = _merge(
        oa, ob, proj, x2, w_branch_a[0].astype(BF16), w_branch_b[0].astype(BF16),
        w_out[0].astype(BF16), norm2[0][None, :], w_pq[0].astype(BF16), sub_keys[0].astype(BF16))

    out = _peer(ht, e1, cnt, e2, rk, expert_u[0].astype(BF16), expert_v[0].T.astype(BF16), x1,
                final_norm[None, :])
    return out.reshape(bsz, t, D_MODEL)
```

```python
import functools

import jax
import jax.numpy as jnp
from jax import lax
from jax.experimental import pallas as pl
from jax.experimental.pallas import tpu as pltpu

F32 = jnp.float32
BF16 = jnp.bfloat16
EPS = 1e-6

D_MODEL = 1024
HEADS = 8
HEAD_DIM = 128
CHUNK = 64
CONV_W = 4
PAD_ROWS = 8
SUB = 16

N_KEYS = 128
P_HEADS = 8
P_TOPK = 16
RANK_NONE = 64.0
MXU_DEPTH = 256

VMEM_LIMIT = 56 * 1024 * 1024

_NT = (((1,), (1,)), ((), ()))
_TN = (((0,), (0,)), ((), ()))


def _mm(a, b):
    return jnp.dot(a.astype(BF16), b.astype(BF16), preferred_element_type=F32)


def _mm_nt(a, b):
    return lax.dot_general(a.astype(BF16), b.astype(BF16), _NT, preferred_element_type=F32)


def _mm_tn(a, b):
    return lax.dot_general(a.astype(BF16), b.astype(BF16), _TN, preferred_element_type=F32)


def _split(a):
    hi = a.astype(BF16)
    return hi, (a - hi.astype(F32)).astype(BF16)


def _mm_solve(a, b):
    a_hi, a_lo = _split(a)
    b_hi, b_lo = _split(b)
    dot = functools.partial(jnp.dot, preferred_element_type=F32)
    return dot(a_hi, b_hi) + (dot(a_hi, b_lo) + dot(a_lo, b_hi))


def _split3(b):
    b_hi, b_lo = _split(b)
    return b_hi, b_lo, (b - b_hi.astype(F32) - b_lo.astype(F32)).astype(BF16)


def _mm_cumsum(tri01, b):
    b_hi, b_lo, b_lo2 = _split3(b)
    t = tri01.astype(BF16)
    dot = functools.partial(jnp.dot, preferred_element_type=F32)
    return dot(t, b_hi) + (dot(t, b_lo) + dot(t, b_lo2))


def _mm_cumsum_rhs(a, tri01):
    a_hi, a_lo, a_lo2 = _split3(a)
    t = tri01.astype(BF16)
    dot = functools.partial(jnp.dot, preferred_element_type=F32)
    return dot(a_hi, t) + (dot(a_lo, t) + dot(a_lo2, t))


def _sigmoid(x):
    return 1.0 / (1.0 + jnp.exp(-x))


def _softplus(x):
    return jnp.maximum(x, 0.0) + jnp.log(1.0 + jnp.exp(-jnp.abs(x)))


def _rms(x, g):
    return x * lax.rsqrt(jnp.mean(x * x, axis=-1, keepdims=True) + EPS) * g


def _tri(n, strict=False):
    r = lax.broadcasted_iota(jnp.int32, (n, n), 0)
    c = lax.broadcasted_iota(jnp.int32, (n, n), 1)
    return (r > c) if strict else (r >= c)


def _inproj_kernel(x_ref, g_ref, w_ref, ws_ref, wst_ref, o_ref, os_ref, ost_ref, h_scr):
    @pl.when(pl.program_id(1) == 0)
    def _():
        hb = _rms(x_ref[...], g_ref[...]).astype(BF16)
        h_scr[...] = hb
        os_ref[...] = jnp.dot(hb, ws_ref[...], preferred_element_type=F32)
        ost_ref[...] = lax.dot_general(wst_ref[...], hb, _NT, preferred_element_type=F32)

    o_ref[...] = jnp.dot(h_scr[...], w_ref[...], preferred_element_type=F32)


def _inproj(x2, g, w_main, w_small, w_small_t, tm=1024, tn=1024):
    nt = x2.shape[0]
    ncol = w_main.shape[1]
    return pl.pallas_call(
        _inproj_kernel,
        grid=(nt // tm, ncol // tn),
        in_specs=[
            pl.BlockSpec((tm, D_MODEL), lambda i, j: (i, 0)),
            pl.BlockSpec((1, D_MODEL), lambda i, j: (0, 0)),
            pl.BlockSpec((D_MODEL, tn), lambda i, j: (0, j)),
            pl.BlockSpec((D_MODEL, 128), lambda i, j: (0, 0)),
            pl.BlockSpec((16, D_MODEL), lambda i, j: (0, 0)),
        ],
        out_specs=[
            pl.BlockSpec((tm, tn), lambda i, j: (i, j)),
            pl.BlockSpec((tm, 128), lambda i, j: (i, 0)),
            pl.BlockSpec((16, tm), lambda i, j: (0, i)),
        ],
        out_shape=[
            jax.ShapeDtypeStruct((nt, ncol), F32),
            jax.ShapeDtypeStruct((nt, 128), F32),
            jax.ShapeDtypeStruct((16, nt), F32),
        ],
        scratch_shapes=[pltpu.VMEM((tm, D_MODEL), BF16)],
        compiler_params=pltpu.CompilerParams(
            dimension_semantics=("arbitrary", "arbitrary"), vmem_limit_bytes=VMEM_LIMIT),
        name="inproj",
    )(x2, g, w_main, w_small, w_small_t)


def _gdn_kernel(q_ref, k_ref, v_ref, gate_ref, sm_ref, *rest, tblk, bsz):
    smt_refs = rest[:bsz]
    (conv_ref, alog_r, dtb_r, alog_c, dtb_c, onorm_ref, o_ref,
     s_scr, xpad_scr, qkv_scr, grow_scr) = rest[bsz:]
    nc = tblk // CHUNK

    @pl.when(pl.program_id(0) == 0)
    def _():
        s_scr[...] = jnp.zeros_like(s_scr)
        xpad_scr[:, 0:PAD_ROWS, :] = jnp.zeros((bsz, PAD_ROWS, 3 * D_MODEL), F32)

    r = lax.broadcasted_iota(jnp.int32, (tblk, tblk), 0)
    c = lax.broadcasted_iota(jnp.int32, (tblk, tblk), 1)
    lg_chunk = CHUNK.bit_length() - 1
    same_chunk = jnp.right_shift(r, lg_chunk) == jnp.right_shift(c, lg_chunk)
    ublk = jnp.where(r <= c, jnp.where(same_chunk, 1.0, 0.0), 0.0).astype(F32)
    conv_w = conv_ref[...]
    col_grp = 512

    def conv_chunk(ci):
        for b in range(bsz):
            for cg in range(3 * D_MODEL // col_grp):
                cs = slice(cg * col_grp, (cg + 1) * col_grp)
                r0 = PAD_ROWS + ci * CHUNK
                y = xpad_scr[b, r0:r0 + CHUNK, cs] * conv_w[CONV_W - 1:CONV_W, cs]
                for j in range(CONV_W - 1):
                    off = r0 - (CONV_W - 1) + j
                    y = y + xpad_scr[b, off:off + CHUNK, cs] * conv_w[j:j + 1, cs]
                qkv_scr[b, ci * CHUNK:(ci + 1) * CHUNK, cs] = y * _sigmoid(y)

    for b in range(bsz):
        for idx, ref in enumerate((q_ref, k_ref, v_ref)):
            xpad_scr[b, PAD_ROWS:PAD_ROWS + tblk, idx * D_MODEL:(idx + 1) * D_MODEL] = ref[b]

        g_t = -jnp.exp(alog_c[...]) * _softplus(smt_refs[b][...] + dtb_c[...])
        grow = _mm_cumsum_rhs(g_t, ublk)
        for ci in range(nc):
            grow_scr[b, ci] = grow[:, ci * CHUNK:(ci + 1) * CHUNK]

    ltri = jnp.where(_tri(CHUNK), 1.0, 0.0).astype(F32)
    causal = _tri(CHUNK)
    strict = _tri(CHUNK, strict=True)
    onorm = onorm_ref[...]
    alog_row = alog_r[...]
    dtb_row = dtb_r[...]

    def chunk_body(ci):
        rows = slice(ci * CHUNK, (ci + 1) * CHUNK)
        if ci + 1 < nc:
            conv_chunk(ci + 1)
        beta_w, gcol_w, grow_c = [], [], []
        for b in range(bsz):
            sm = sm_ref[b, rows, :]
            beta_w.append(_sigmoid(sm))
            g_w = -jnp.exp(alog_row) * _softplus(sm + dtb_row)
            gcol_w.append(_mm_cumsum(ltri, g_w))
            grow_c.append(grow_scr[b, ci])
        chains = [(b, h) for b in range(bsz) for h in range(HEADS)]
        hs = range(len(chains))
        hcols = [slice(h * HEAD_DIM, (h + 1) * HEAD_DIM) for _, h in chains]

        p, xs, a_qk, q_dec, k_dec, g_last = [], [], [], [], [], []
        for b, h in chains:
            qc = qkv_scr[b, rows, h * HEAD_DIM:(h + 1) * HEAD_DIM]
            kc = qkv_scr[b, rows, D_MODEL + h * HEAD_DIM:D_MODEL + (h + 1) * HEAD_DIM]
            vc = qkv_scr[b, rows, 2 * D_MODEL + h * HEAD_DIM:2 * D_MODEL + (h + 1) * HEAD_DIM]
            q = qc * (lax.rsqrt(jnp.sum(qc * qc, axis=-1, keepdims=True) + EPS) * (HEAD_DIM ** -0.5))
            k = kc * lax.rsqrt(jnp.sum(kc * kc, axis=-1, keepdims=True) + EPS)
            beta = beta_w[b][:, h:h + 1]
            g_c = gcol_w[b][:, HEADS + h:HEADS + h + 1]
            g_r = grow_c[b][HEADS + h:HEADS + h + 1, :]
            gl = g_r[:, CHUNK - 1:CHUNK]
            decay = jnp.where(causal, jnp.exp(jnp.where(causal, g_c - g_r, 0.0)), 0.0)
            eg = jnp.exp(g_c)
            kb = k * beta
            p.append(-jnp.where(strict, _mm_nt(kb, k) * decay, 0.0))
            a_qk.append(jnp.where(causal, _mm_nt(q, k) * decay, 0.0))
            xs.append(jnp.concatenate([vc * beta, kb * eg], axis=-1))
            q_dec.append(q * eg)
            k_dec.append(k * jnp.exp(gl - g_c))
            g_last.append(jnp.exp(gl))

        n_lvl = CHUNK.bit_length() - 1
        for lvl in range(n_lvl):
            last = lvl == n_lvl - 1
            for h in hs:
                rhs = xs[h] if last else jnp.concatenate([xs[h], p[h]], axis=-1)
                r = _mm(p[h], rhs)
                xs[h] = xs[h] + r[:, :2 * HEAD_DIM]
                if not last:
                    p[h] = r[:, 2 * HEAD_DIM:]

        s_old = [s_scr[h] for h in hs]
        v_new = [xs[h][:, :HEAD_DIM] - _mm(xs[h][:, HEAD_DIM:], s_old[h]) for h in hs]
        for ch, (b, h) in enumerate(chains):
            o = _mm(q_dec[ch], s_old[ch]) + _mm(a_qk[ch], v_new[ch])
            s_scr[ch] = s_old[ch] * g_last[ch] + _mm_tn(k_dec[ch], v_new[ch])
            gate = gate_ref[b, rows, hcols[ch]]
            o_ref[b, rows, hcols[ch]] = (
                _rms(o, onorm) * (gate * _sigmoid(gate))).astype(o_ref.dtype)

    conv_chunk(0)
    for ci in range(nc):
        chunk_body(ci)
    for b in range(bsz):
        xpad_scr[b, 0:PAD_ROWS, :] = xpad_scr[b, tblk:tblk + PAD_ROWS, :]


def _gdn(proj, small, small_t, conv, alog_r, dtb_r, alog_c, dtb_c, onorm, bsz, t, tblk=256):
    nb = t // tblk
    blk = lambda col: pl.BlockSpec((bsz, tblk, D_MODEL), lambda i: (0, i, col))
    full = lambda a: pl.BlockSpec(a.shape, lambda i: (0,) * a.ndim)
    smt_specs = [pl.BlockSpec((16, tblk), functools.partial(lambda i, b: (0, b * nb + i), b=b))
                 for b in range(bsz)]
    return pl.pallas_call(
        functools.partial(_gdn_kernel, tblk=tblk, bsz=bsz),
        grid=(nb,),
        in_specs=[blk(0), blk(1), blk(2), blk(3),
                  pl.BlockSpec((bsz, tblk, 128), lambda i: (0, i, 0))] + smt_specs + [
                  full(conv), full(alog_r), full(dtb_r), full(alog_c), full(dtb_c), full(onorm)],
        out_specs=pl.BlockSpec((bsz, tblk, D_MODEL), lambda i: (0, i, 0)),
        out_shape=jax.ShapeDtypeStruct((bsz, t, D_MODEL), BF16),
        scratch_shapes=[
            pltpu.VMEM((bsz * HEADS, HEAD_DIM, HEAD_DIM), F32),
            pltpu.VMEM((bsz, PAD_ROWS + tblk, 3 * D_MODEL), F32),
            pltpu.VMEM((bsz, tblk, 3 * D_MODEL), F32),
            pltpu.VMEM((bsz, tblk // CHUNK, 16, CHUNK), F32),
        ],
        compiler_params=pltpu.CompilerParams(
            dimension_semantics=("arbitrary",), vmem_limit_bytes=VMEM_LIMIT),
        name="gdn",
    )(proj, proj, proj, proj, small, *([small_t] * bsz), conv, alog_r, dtb_r, alog_c, dtb_c, onorm)


def _hgrn_kernel(f_ref, i_ref, q_ref, gate_ref, lb_ref, onorm_ref, o_ref, st_scr, *, tblk, bsz):
    nc = tblk // CHUNK

    @pl.when(pl.program_id(0) == 0)
    def _():
        st_scr[...] = jnp.zeros_like(st_scr)

    lbp = lb_ref[...]
    e = jnp.exp(lbp - jnp.max(lbp, axis=0, keepdims=True))
    lb = e[0:1] / jnp.sum(e, axis=0, keepdims=True)
    ltri = jnp.where(_tri(CHUNK), 1.0, 0.0).astype(F32)
    onorm = onorm_ref[...]

    def chunk_body(ci, carry):
        rows = pl.ds(pl.multiple_of(ci * CHUNK, CHUNK), CHUNK)
        lf_b, b_b, q_b, k_b, eb_b = [], [], [], [], []
        for bi in range(bsz):
            f_all = lb + (1.0 - lb) * _sigmoid(f_ref[bi, rows, :])
            lf_all = jnp.log(f_all)
            b_all = _mm_cumsum(ltri, lf_all)
            q_all = q_ref[bi, rows, :]
            lf_b.append(lf_all)
            b_b.append(b_all)
            q_b.append(q_all * _sigmoid(q_all))
            k_b.append(1.0 - f_all)
            eb_b.append(jnp.exp(b_all))
        chains = [(bi, h) for bi in range(bsz) for h in range(HEADS)]
        hs = range(len(chains))
        hcols = [slice(h * HEAD_DIM, (h + 1) * HEAD_DIM) for _, h in chains]
        n_sub = CHUNK // SUB

        a_blk = []
        for ch, (bi, _) in enumerate(chains):
            b = b_b[bi][:, hcols[ch]]
            lf = lf_b[bi][:, hcols[ch]]
            kk = k_b[bi][:, hcols[ch]]
            q = q_b[bi][:, hcols[ch]]
            blks = []
            for blk in range(n_sub):
                r0 = blk * SUB
                n = r0 + SUB
                ref = b[r0:r0 + 1, :] - lf[r0:r0 + 1, :]
                qt = q[r0:n] * jnp.exp(b[r0:n] - ref)
                kt = kk[:n] * jnp.exp(ref - b[:n])
                a = _mm_nt(qt, kt)
                rr = lax.broadcasted_iota(jnp.int32, (SUB, n), 0)
                cc = lax.broadcasted_iota(jnp.int32, (SUB, n), 1)
                blks.append(jnp.where(cc <= rr + r0, a, 0.0))
            a_blk.append(blks)
        st_old = [st_scr[ch] for ch in hs]
        outs = []
        for ch, (bi, _) in enumerate(chains):
            v = i_ref[bi, rows, hcols[ch]]
            intra = [_mm(a_blk[ch][blk], v[:(blk + 1) * SUB]) for blk in range(n_sub)]
            inter = _mm_nt(q_b[bi][:, hcols[ch]] * eb_b[bi][:, hcols[ch]], st_old[ch])
            outs.append(jnp.concatenate(intra, axis=0) + inter)
        for ch, (bi, _) in enumerate(chains):
            b = b_b[bi][:, hcols[ch]]
            b_last = b[CHUNK - 1:CHUNK, :]
            v = i_ref[bi, rows, hcols[ch]]
            st_scr[ch] = (st_old[ch] * jnp.exp(b_last)
                          + _mm_tn(v, k_b[bi][:, hcols[ch]] * jnp.exp(b_last - b)))
            gate = gate_ref[bi, rows, hcols[ch]]
            o_ref[bi, rows, hcols[ch]] = (
                _rms(outs[ch], onorm) * _sigmoid(gate)).astype(o_ref.dtype)
        return carry

    lax.fori_loop(0, nc, chunk_body, 0)


def _hgrn(proj, lbp, onorm, bsz, t, tblk=256):
    nb = t // tblk
    blk = lambda col: pl.BlockSpec((bsz, tblk, D_MODEL), lambda i: (0, i, col))
    full = lambda a: pl.BlockSpec(a.shape, lambda i: (0,) * a.ndim)
    return pl.pallas_call(
        functools.partial(_hgrn_kernel, tblk=tblk, bsz=bsz),
        grid=(nb,),
        in_specs=[blk(4), blk(5), blk(6), blk(7), full(lbp), full(onorm)],
        out_specs=pl.BlockSpec((bsz, tblk, D_MODEL), lambda i: (0, i, 0)),
        out_shape=jax.ShapeDtypeStruct((bsz, t, D_MODEL), BF16),
        scratch_shapes=[pltpu.VMEM((bsz * HEADS, HEAD_DIM, HEAD_DIM), F32)],
        compiler_params=pltpu.CompilerParams(
            dimension_semantics=("arbitrary",), vmem_limit_bytes=VMEM_LIMIT),
        name="hgrn2",
    )(proj, proj, proj, proj, lbp, onorm)


def _cand_pairs():
    return [(a, b) for a in range(P_TOPK) for b in range(P_TOPK) if (a + 1) * (b + 1) <= P_TOPK]


def _extract_top(s, n, with_rank=False):
    tops = []
    cur = s
    rank = jnp.full(s.shape, RANK_NONE, F32) if with_rank else None
    for r in range(n):
        m = jnp.max(cur, axis=0, keepdims=True)
        tops.append(m)
        hit = cur == m
        if with_rank:
            rank = jnp.where(hit, float(r), rank)
        cur = jnp.where(hit, -jnp.inf, cur)
    return (tops, rank) if with_rank else tops


def _count_kept(s1, top2, tau):
    n = len(top2)
    assert n & (n - 1) == 0

    def row(masks, lo, step):
        if not masks:
            return top2[lo + step - 1]
        weight = step << len(masks)
        return jnp.where(masks[0], row(masks[1:], lo + weight, step), row(masks[1:], lo, step))

    masks, weights = [], []
    step = n // 2
    while step >= 1:
        masks.append(s1 + row(masks, 0, step) >= tau)
        weights.append(float(step))
        step //= 2
    masks.append(s1 + top2[n - 1] >= tau)
    weights.append(1.0)
    cnt = jnp.where(masks[0], weights[0], 0.0)
    for m, w in zip(masks[1:], weights[1:]):
        cnt = cnt + jnp.where(m, w, 0.0)
    return cnt


def _merge_kernel(oa_ref, ob_ref, ga_ref, gb_ref, x_ref, wa_ref, wb_ref, wo_ref, n2_ref,
                  wpq_ref, keys_ref, x1_ref, ht_ref, e1_ref, cnt_ref, e2_ref, rk_ref):
    ya = jnp.dot(oa_ref[...], wa_ref[...], preferred_element_type=F32)
    yb = jnp.dot(ob_ref[...], wb_ref[...], preferred_element_type=F32)
    mix = _sigmoid(ga_ref[...]) * ya + _sigmoid(gb_ref[...]) * yb
    x1 = x_ref[...] + _mm(mix, wo_ref[...])
    x1_ref[...] = x1
    h2 = _rms(x1, n2_ref[...])
    ht_ref[...] = h2.T.astype(BF16)
    qp = _mm(h2, wpq_ref[...])
    pairs = _cand_pairs()
    for h in range(P_HEADS):
        c0 = h * 2 * HEAD_DIM
        s1 = _mm_nt(keys_ref[0, h], qp[:, c0:c0 + HEAD_DIM])
        s2 = _mm_nt(keys_ref[1, h], qp[:, c0 + HEAD_DIM:c0 + 2 * HEAD_DIM])
        top1 = _extract_top(s1, P_TOPK)
        top2, rank2 = _extract_top(s2, P_TOPK, with_rank=True)
        cand = jnp.concatenate([top1[a] + top2[b] for a, b in pairs], axis=0)
        ctop = _extract_top(cand, P_TOPK)
        c_max = ctop[0]
        tau = ctop[P_TOPK - 1]
        z = jnp.exp(ctop[0] - c_max)
        for r_ in range(1, P_TOPK):
            z = z + jnp.exp(ctop[r_] - c_max)
        inv_z = 1.0 / z
        in1 = s1 >= top1[P_TOPK - 1]
        cnt = _count_kept(s1, top2, tau)
        e1_ref[h] = jnp.where(in1, jnp.exp(s1 - top1[0]), 0.0)
        cnt_ref[h] = jnp.where(in1, cnt, 0.0)
        e2_ref[h] = (jnp.where(rank2 < RANK_NONE, jnp.exp(s2 - top2[0]), 0.0) * inv_z).astype(BF16)
        rk_ref[h] = rank2.astype(BF16)


def _merge(oa, ob, proj, x2, wa, wb, wo, n2, wpq, keys, tb=256):
    nt = x2.shape[0]
    rowblk = lambda col: pl.BlockSpec((tb, D_MODEL), lambda i: (i, col))
    full = lambda a: pl.BlockSpec(a.shape, lambda i: (0,) * a.ndim)
    return pl.pallas_call(
        _merge_kernel,
        grid=(nt // tb,),
        in_specs=[rowblk(0), rowblk(0), rowblk(8), rowblk(9), rowblk(0),
                  full(wa), full(wb), full(wo), full(n2), full(wpq), full(keys)],
        out_specs=[
            pl.BlockSpec((tb, D_MODEL), lambda i: (i, 0)),
            pl.BlockSpec((D_MODEL, tb), lambda i: (0, i)),
            pl.BlockSpec((P_HEADS, N_KEYS, tb), lambda i: (0, 0, i)),
            pl.BlockSpec((P_HEADS, N_KEYS, tb), lambda i: (0, 0, i)),
            pl.BlockSpec((P_HEADS, N_KEYS, tb), lambda i: (0, 0, i)),
            pl.BlockSpec((P_HEADS, N_KEYS, tb), lambda i: (0, 0, i)),
        ],
        out_shape=[
            jax.ShapeDtypeStruct((nt, D_MODEL), F32),
            jax.ShapeDtypeStruct((D_MODEL, nt), BF16),
            jax.ShapeDtypeStruct((P_HEADS, N_KEYS, nt), F32),
            jax.ShapeDtypeStruct((P_HEADS, N_KEYS, nt), F32),
            jax.ShapeDtypeStruct((P_HEADS, N_KEYS, nt), BF16),
            jax.ShapeDtypeStruct((P_HEADS, N_KEYS, nt), BF16),
        ],
        compiler_params=pltpu.CompilerParams(
            dimension_semantics=("arbitrary",), vmem_limit_bytes=VMEM_LIMIT),
        name="merge",
    )(oa, ob, proj, proj, x2, wa, wb, wo, n2, wpq, keys)


def _peer_kernel(ht_ref, e1_ref, cnt_ref, e2_ref, rk_ref, u_ref, vt_ref, x1_ref, fn_ref, o_ref,
                 acc_scr, *, eb):
    j = pl.program_id(1)
    tb = ht_ref.shape[1]

    @pl.when(j == 0)
    def _():
        acc_scr[...] = jnp.zeros_like(acc_scr)

    ht = ht_ref[...]
    per = MXU_DEPTH // N_KEYS
    n_grp = eb // MXU_DEPTH
    grp_rows = [slice(g * MXU_DEPTH, (g + 1) * MXU_DEPTH) for g in range(n_grp)]
    hid_next = jnp.dot(u_ref[grp_rows[0], :], ht, preferred_element_type=F32)
    gated_prev = None
    for g in range(n_grp):
        hid = hid_next
        if g + 1 < n_grp:
            hid_next = jnp.dot(u_ref[grp_rows[g + 1], :], ht, preferred_element_type=F32)
        if gated_prev is not None:
            acc_scr[...] += jnp.dot(vt_ref[:, grp_rows[g - 1]], gated_prev,
                                    preferred_element_type=F32)
        parts = []
        for sub in range(per):
            i1 = j * (eb // N_KEYS) + g * per + sub
            wgt = None
            for h in range(P_HEADS):
                e1 = jnp.broadcast_to(e1_ref[h, pl.ds(i1, 1), :], (N_KEYS, tb)).astype(BF16)
                cnt = jnp.broadcast_to(cnt_ref[h, pl.ds(i1, 1), :], (N_KEYS, tb)).astype(BF16)
                term = jnp.where(rk_ref[h] < cnt, e2_ref[h] * e1, jnp.zeros((), BF16))
                wgt = term if wgt is None else wgt + term
            parts.append(wgt)
        wgt = jnp.concatenate(parts, axis=0)
        hb = hid.astype(BF16)
        half = hb * jnp.asarray(0.5, BF16)
        act = half * lax.erf(hb * jnp.asarray(2.0 ** -0.5, BF16)) + half
        gated_prev = wgt * act
    acc_scr[...] += jnp.dot(vt_ref[:, grp_rows[n_grp - 1]], gated_prev,
                            preferred_element_type=F32)

    @pl.when(j == pl.num_programs(1) - 1)
    def _():
        o_ref[...] = _rms(x1_ref[...] + acc_scr[...].T, fn_ref[...])


def _peer(ht, e1, cnt, e2, rk, u, vt, x1, fn, tb=512, eb=2048):
    nt = x1.shape[0]
    ne = u.shape[0]
    return pl.pallas_call(
        functools.partial(_peer_kernel, eb=eb),
        grid=(nt // tb, ne // eb),
        in_specs=[
            pl.BlockSpec((D_MODEL, tb), lambda i, j: (0, i)),
            pl.BlockSpec((P_HEADS, N_KEYS, tb), lambda i, j: (0, 0, i)),
            pl.BlockSpec((P_HEADS, N_KEYS, tb), lambda i, j: (0, 0, i)),
            pl.BlockSpec((P_HEADS, N_KEYS, tb), lambda i, j: (0, 0, i)),
            pl.BlockSpec((P_HEADS, N_KEYS, tb), lambda i, j: (0, 0, i)),
            pl.BlockSpec((eb, D_MODEL), lambda i, j: (j, 0)),
            pl.BlockSpec((D_MODEL, eb), lambda i, j: (0, j)),
            pl.BlockSpec((tb, D_MODEL), lambda i, j: (i, 0)),
            pl.BlockSpec((1, D_MODEL), lambda i, j: (0, 0)),
        ],
        out_specs=pl.BlockSpec((tb, D_MODEL), lambda i, j: (i, 0)),
        out_shape=jax.ShapeDtypeStruct((nt, D_MODEL), F32),
        scratch_shapes=[pltpu.VMEM((D_MODEL, tb), F32)],
        compiler_params=pltpu.CompilerParams(
            dimension_semantics=("arbitrary", "arbitrary"), vmem_limit_bytes=VMEM_LIMIT),
        name="peer",
    )(ht, e1, cnt, e2, rk, u, vt, x1, fn)


def _pad_lanes(row, offset):
    return jnp.zeros((1, 128), F32).at[0, offset:offset + row.shape[0]].set(row.astype(F32))


def kernel(x, norm1, w_in, conv_a, a_log, dt_bias, a_onorm, b_lower_bound, b_onorm, w_branch_a,
           w_branch_b, w_out, norm2, w_pq, sub_keys, expert_u, expert_v, final_norm):
    assert norm1.shape[0] == 1, "one layer"
    bsz, t, _ = x.shape
    nt = bsz * t
    x2 = x.reshape(nt, D_MODEL)

    wi = w_in[0]
    n_qkv = 3 * D_MODEL
    n_used = 10 * D_MODEL + 2 * HEADS
    w_main = jnp.concatenate([wi[:, :n_qkv], wi[:, n_qkv + 2 * HEADS:n_used]], axis=1).astype(BF16)
    w_small = jnp.pad(wi[:, n_qkv:n_qkv + 2 * HEADS], ((0, 0), (0, 128 - 2 * HEADS))).astype(BF16)
    w_small_t = wi[:, n_qkv:n_qkv + 2 * HEADS].T.astype(BF16)

    proj, small, small_t = _inproj(x2, norm1[0][None, :], w_main, w_small, w_small_t)

    alog_r = _pad_lanes(a_log[0], HEADS)
    dtb_r = _pad_lanes(dt_bias[0], HEADS)
    alog_c = alog_r[0, :16][:, None]
    dtb_c = dtb_r[0, :16][:, None]
    oa = _gdn(proj.reshape(bsz, t, -1), small.reshape(bsz, t, -1), small_t, conv_a[0], alog_r,
              dtb_r, alog_c, dtb_c, a_onorm[0][None, :], bsz, t).reshape(nt, D_MODEL)
    ob = _hgrn(proj.reshape(bsz, t, -1), b_lower_bound, b_onorm[0][None, :], bsz,
               t).reshape(nt, D_MODEL)

    x1, ht, e1, cnt, e2, rk = _merge(
        oa, ob, proj, x2, w_branch_a[0].astype(BF16), w_branch_b[0].astype(BF16),
        w_out[0].astype(BF16), norm2[0][None, :], w_pq[0].astype(BF16), sub_keys[0].astype(BF16))

    out = _peer(ht, e1, cnt, e2, rk, expert_u[0].astype(BF16), expert_v[0].T.astype(BF16), x1,
                final_norm[None, :])
    return out.reshape(bsz, t, D_MODEL)
```

```python
import functools

import jax
import jax.numpy as jnp
from jax import lax
from jax.experimental import pallas as pl
from jax.experimental.pallas import tpu as pltpu

F32 = jnp.float32
BF16 = jnp.bfloat16
EPS = 1e-6

D_MODEL = 1024
HEADS = 8
HEAD_DIM = 128
CHUNK = 64
CONV_W = 4
PAD_ROWS = 8
SUB = 16

N_KEYS = 128
P_HEADS = 8
P_TOPK = 16
RANK_NONE = 64.0
MXU_DEPTH = 256

VMEM_LIMIT = 56 * 1024 * 1024

_NT = (((1,), (1,)), ((), ()))
_TN = (((0,), (0,)), ((), ()))


def _mm(a, b):
    return jnp.dot(a.astype(BF16), b.astype(BF16), preferred_element_type=F32)


def _mm_nt(a, b):
    return lax.dot_general(a.astype(BF16), b.astype(BF16), _NT, preferred_element_type=F32)


def _mm_tn(a, b):
    return lax.dot_general(a.astype(BF16), b.astype(BF16), _TN, preferred_element_type=F32)


def _split(a):
    hi = a.astype(BF16)
    return hi, (a - hi.astype(F32)).astype(BF16)


def _mm_solve(a, b):
    a_hi, a_lo = _split(a)
    b_hi, b_lo = _split(b)
    dot = functools.partial(jnp.dot, preferred_element_type=F32)
    return dot(a_hi, b_hi) + (dot(a_hi, b_lo) + dot(a_lo, b_hi))


def _split3(b):
    b_hi, b_lo = _split(b)
    return b_hi, b_lo, (b - b_hi.astype(F32) - b_lo.astype(F32)).astype(BF16)


def _mm_cumsum(tri01, b):
    b_hi, b_lo, b_lo2 = _split3(b)
    t = tri01.astype(BF16)
    dot = functools.partial(jnp.dot, preferred_element_type=F32)
    return dot(t, b_hi) + (dot(t, b_lo) + dot(t, b_lo2))


def _mm_cumsum_rhs(a, tri01):
    a_hi, a_lo, a_lo2 = _split3(a)
    t = tri01.astype(BF16)
    dot = functools.partial(jnp.dot, preferred_element_type=F32)
    return dot(a_hi, t) + (dot(a_lo, t) + dot(a_lo2, t))


def _sigmoid(x):
    return 1.0 / (1.0 + jnp.exp(-x))


def _softplus(x):
    return jnp.maximum(x, 0.0) + jnp.log(1.0 + jnp.exp(-jnp.abs(x)))


def _rms(x, g):
    return x * lax.rsqrt(jnp.mean(x * x, axis=-1, keepdims=True) + EPS) * g


def _tri(n, strict=False):
    r = lax.broadcasted_iota(jnp.int32, (n, n), 0)
    c = lax.broadcasted_iota(jnp.int32, (n, n), 1)
    return (r > c) if strict else (r >= c)


def _inproj_kernel(x_ref, g_ref, w_ref, ws_ref, wst_ref, o_ref, os_ref, ost_ref, h_scr):
    @pl.when(pl.program_id(1) == 0)
    def _():
        hb = _rms(x_ref[...], g_ref[...]).astype(BF16)
        h_scr[...] = hb
        os_ref[...] = jnp.dot(hb, ws_ref[...], preferred_element_type=F32)
        ost_ref[...] = lax.dot_general(wst_ref[...], hb, _NT, preferred_element_type=F32)

    o_ref[...] = jnp.dot(h_scr[...], w_ref[...], preferred_element_type=F32)


def _inproj(x2, g, w_main, w_small, w_small_t, tm=1024, tn=1024):
    nt = x2.shape[0]
    ncol = w_main.shape[1]
    return pl.pallas_call(
        _inproj_kernel,
        grid=(nt // tm, ncol // tn),
        in_specs=[
            pl.BlockSpec((tm, D_MODEL), lambda i, j: (i, 0)),
            pl.BlockSpec((1, D_MODEL), lambda i, j: (0, 0)),
            pl.BlockSpec((D_MODEL, tn), lambda i, j: (0, j)),
            pl.BlockSpec((D_MODEL, 128), lambda i, j: (0, 0)),
            pl.BlockSpec((16, D_MODEL), lambda i, j: (0, 0)),
        ],
        out_specs=[
            pl.BlockSpec((tm, tn), lambda i, j: (i, j)),
            pl.BlockSpec((tm, 128), lambda i, j: (i, 0)),
            pl.BlockSpec((16, tm), lambda i, j: (0, i)),
        ],
        out_shape=[
            jax.ShapeDtypeStruct((nt, ncol), F32),
            jax.ShapeDtypeStruct((nt, 128), F32),
            jax.ShapeDtypeStruct((16, nt), F32),
        ],
        scratch_shapes=[pltpu.VMEM((tm, D_MODEL), BF16)],
        compiler_params=pltpu.CompilerParams(
            dimension_semantics=("arbitrary", "arbitrary"), vmem_limit_bytes=VMEM_LIMIT),
        name="inproj",
    )(x2, g, w_main, w_small, w_small_t)


def _gdn_kernel(q_ref, k_ref, v_ref, gate_ref, sm_ref, *rest, tblk, bsz):
    smt_refs = rest[:bsz]
    (conv_ref, alog_r, dtb_r, alog_c, dtb_c, onorm_ref, o_ref,
     s_scr, xpad_scr, qkv_scr, grow_scr) = rest[bsz:]
    nc = tblk // CHUNK

    @pl.when(pl.program_id(0) == 0)
    def _():
        s_scr[...] = jnp.zeros_like(s_scr)
        xpad_scr[:, 0:PAD_ROWS, :] = jnp.zeros((bsz, PAD_ROWS, 3 * D_MODEL), F32)

    r = lax.broadcasted_iota(jnp.int32, (tblk, tblk), 0)
    c = lax.broadcasted_iota(jnp.int32, (tblk, tblk), 1)
    lg_chunk = CHUNK.bit_length() - 1
    same_chunk = jnp.right_shift(r, lg_chunk) == jnp.right_shift(c, lg_chunk)
    ublk = jnp.where(r <= c, jnp.where(same_chunk, 1.0, 0.0), 0.0).astype(F32)
    conv_w = conv_ref[...]
    col_grp = 512

    def conv_chunk(ci):
        for b in range(bsz):
            for cg in range(3 * D_MODEL // col_grp):
                cs = slice(cg * col_grp, (cg + 1) * col_grp)
                r0 = PAD_ROWS + ci * CHUNK
                y = xpad_scr[b, r0:r0 + CHUNK, cs] * conv_w[CONV_W - 1:CONV_W, cs]
                for j in range(CONV_W - 1):
                    off = r0 - (CONV_W - 1) + j
                    y = y + xpad_scr[b, off:off + CHUNK, cs] * conv_w[j:j + 1, cs]
                qkv_scr[b, ci * CHUNK:(ci + 1) * CHUNK, cs] = y * _sigmoid(y)

    for b in range(bsz):
        for idx, ref in enumerate((q_ref, k_ref, v_ref)):
            xpad_scr[b, PAD_ROWS:PAD_ROWS + tblk, idx * D_MODEL:(idx + 1) * D_MODEL] = ref[b]

        g_t = -jnp.exp(alog_c[...]) * _softplus(smt_refs[b][...] + dtb_c[...])
        grow = _mm_cumsum_rhs(g_t, ublk)
        for ci in range(nc):
            grow_scr[b, ci] = grow[:, ci * CHUNK:(ci + 1) * CHUNK]

    ltri = jnp.where(_tri(CHUNK), 1.0, 0.0).astype(F32)
    causal = _tri(CHUNK)
    strict = _tri(CHUNK, strict=True)
    onorm = onorm_ref[...]
    alog_row = alog_r[...]
    dtb_row = dtb_r[...]

    def chunk_body(ci):
        rows = slice(ci * CHUNK, (ci + 1) * CHUNK)
        if ci + 1 < nc:
            conv_chunk(ci + 1)
        beta_w, gcol_w, grow_c = [], [], []
        for b in range(bsz):
            sm = sm_ref[b, rows, :]
            beta_w.append(_sigmoid(sm))
            g_w = -jnp.exp(alog_row) * _softplus(sm + dtb_row)
            gcol_w.append(_mm_cumsum(ltri, g_w))
            grow_c.append(grow_scr[b, ci])
        chains = [(b, h) for b in range(bsz) for h in range(HEADS)]
        hs = range(len(chains))
        hcols = [slice(h * HEAD_DIM, (h + 1) * HEAD_DIM) for _, h in chains]

        p, xs, a_qk, q_dec, k_dec, g_last = [], [], [], [], [], []
        for b, h in chains:
            qc = qkv_scr[b, rows, h * HEAD_DIM:(h + 1) * HEAD_DIM]
            kc = qkv_scr[b, rows, D_MODEL + h * HEAD_DIM:D_MODEL + (h + 1) * HEAD_DIM]
            vc = qkv_scr[b, rows, 2 * D_MODEL + h * HEAD_DIM:2 * D_MODEL + (h + 1) * HEAD_DIM]
            q = qc * (lax.rsqrt(jnp.sum(qc * qc, axis=-1, keepdims=True) + EPS) * (HEAD_DIM ** -0.5))
            k = kc * lax.rsqrt(jnp.sum(kc * kc, axis=-1, keepdims=True) + EPS)
            beta = beta_w[b][:, h:h + 1]
            g_c = gcol_w[b][:, HEADS + h:HEADS + h + 1]
            g_r = grow_c[b][HEADS + h:HEADS + h + 1, :]
            gl = g_r[:, CHUNK - 1:CHUNK]
            decay = jnp.where(causal, jnp.exp(jnp.where(causal, g_c - g_r, 0.0)), 0.0)
            eg = jnp.exp(g_c)
            kb = k * beta
            p.append(-jnp.where(strict, _mm_nt(kb, k) * decay, 0.0))
            a_qk.append(jnp.where(causal, _mm_nt(q, k) * decay, 0.0))
            xs.append(jnp.concatenate([vc * beta, kb * eg], axis=-1))
            q_dec.append(q * eg)
            k_dec.append(k * jnp.exp(gl - g_c))
            g_last.append(jnp.exp(gl))

        n_lvl = CHUNK.bit_length() - 1
        for lvl in range(n_lvl):
            last = lvl == n_lvl - 1
            for h in hs:
                rhs = xs[h] if last else jnp.concatenate([xs[h], p[h]], axis=-1)
                r = _mm(p[h], rhs)
                xs[h] = xs[h] + r[:, :2 * HEAD_DIM]
                if not last:
                    p[h] = r[:, 2 * HEAD_DIM:]

        s_old = [s_scr[h] for h in hs]
        v_new = [xs[h][:, :HEAD_DIM] - _mm(xs[h][:, HEAD_DIM:], s_old[h]) for h in hs]
        for ch, (b, h) in enumerate(chains):
            o = _mm(q_dec[ch], s_old[ch]) + _mm(a_qk[ch], v_new[ch])
            s_scr[ch] = s_old[ch] * g_last[ch] + _mm_tn(k_dec[ch], v_new[ch])
            gate = gate_ref[b, rows, hcols[ch]]
            o_ref[b, rows, hcols[ch]] = (
                _rms(o, onorm) * (gate * _sigmoid(gate))).astype(o_ref.dtype)

    conv_chunk(0)
    for ci in range(nc):
        chunk_body(ci)
    for b in range(bsz):
        xpad_scr[b, 0:PAD_ROWS, :] = xpad_scr[b, tblk:tblk + PAD_ROWS, :]


def _gdn(proj, small, small_t, conv, alog_r, dtb_r, alog_c, dtb_c, onorm, bsz, t, tblk=256):
    nb = t // tblk
    blk = lambda col: pl.BlockSpec((bsz, tblk, D_MODEL), lambda i: (0, i, col))
    full = lambda a: pl.BlockSpec(a.shape, lambda i: (0,) * a.ndim)
    smt_specs = [pl.BlockSpec((16, tblk), functools.partial(lambda i, b: (0, b * nb + i), b=b))
                 for b in range(bsz)]
    return pl.pallas_call(
        functools.partial(_gdn_kernel, tblk=tblk, bsz=bsz),
        grid=(nb,),
        in_specs=[blk(0), blk(1), blk(2), blk(3),
                  pl.BlockSpec((bsz, tblk, 128), lambda i: (0, i, 0))] + smt_specs + [
                  full(conv), full(alog_r), full(dtb_r), full(alog_c), full(dtb_c), full(onorm)],
        out_specs=pl.BlockSpec((bsz, tblk, D_MODEL), lambda i: (0, i, 0)),
        out_shape=jax.ShapeDtypeStruct((bsz, t, D_MODEL), BF16),
        scratch_shapes=[
            pltpu.VMEM((bsz * HEADS, HEAD_DIM, HEAD_DIM), F32),
            pltpu.VMEM((bsz, PAD_ROWS + tblk, 3 * D_MODEL), F32),
            pltpu.VMEM((bsz, tblk, 3 * D_MODEL), F32),
            pltpu.VMEM((bsz, tblk // CHUNK, 16, CHUNK), F32),
        ],
        compiler_params=pltpu.CompilerParams(
            dimension_semantics=("arbitrary",), vmem_limit_bytes=VMEM_LIMIT),
        name="gdn",
    )(proj, proj, proj, proj, small, *([small_t] * bsz), conv, alog_r, dtb_r, alog_c, dtb_c, onorm)


def _hgrn_kernel(f_ref, i_ref, q_ref, gate_ref, lb_ref, onorm_ref, o_ref, st_scr, *, tblk, bsz):
    nc = tblk // CHUNK

    @pl.when(pl.program_id(0) == 0)
    def _():
        st_scr[...] = jnp.zeros_like(st_scr)

    lbp = lb_ref[...]
    e = jnp.exp(lbp - jnp.max(lbp, axis=0, keepdims=True))
    lb = e[0:1] / jnp.sum(e, axis=0, keepdims=True)
    ltri = jnp.where(_tri(CHUNK), 1.0, 0.0).astype(F32)
    onorm = onorm_ref[...]

    def chunk_body(ci, carry):
        rows = pl.ds(pl.multiple_of(ci * CHUNK, CHUNK), CHUNK)
        lf_b, b_b, q_b, k_b, eb_b = [], [], [], [], []
        for bi in range(bsz):
            f_all = lb + (1.0 - lb) * _sigmoid(f_ref[bi, rows, :])
            lf_all = jnp.log(f_all)
            b_all = _mm_cumsum(ltri, lf_all)
            q_all = q_ref[bi, rows, :]
            lf_b.append(lf_all)
            b_b.append(b_all)
            q_b.append(q_all * _sigmoid(q_all))
            k_b.append(1.0 - f_all)
            eb_b.append(jnp.exp(b_all))
        chains = [(bi, h) for bi in range(bsz) for h in range(HEADS)]
        hs = range(len(chains))
        hcols = [slice(h * HEAD_DIM, (h + 1) * HEAD_DIM) for _, h in chains]
        n_sub = CHUNK // SUB

        a_blk = []
        for ch, (bi, _) in enumerate(chains):
            b = b_b[bi][:, hcols[ch]]
            lf = lf_b[bi][:, hcols[ch]]
            kk = k_b[bi][:, hcols[ch]]
            q = q_b[bi][:, hcols[ch]]
            blks = []
            for blk in range(n_sub):
                r0 = blk * SUB
                n = r0 + SUB
                ref = b[r0:r0 + 1, :] - lf[r0:r0 + 1, :]
                qt = q[r0:n] * jnp.exp(b[r0:n] - ref)
                kt = kk[:n] * jnp.exp(ref - b[:n])
                a = _mm_nt(qt, kt)
                rr = lax.broadcasted_iota(jnp.int32, (SUB, n), 0)
                cc = lax.broadcasted_iota(jnp.int32, (SUB, n), 1)
                blks.append(jnp.where(cc <= rr + r0, a, 0.0))
            a_blk.append(blks)
        st_old = [st_scr[ch] for ch in hs]
        outs = []
        for ch, (bi, _) in enumerate(chains):
            v = i_ref[bi, rows, hcols[ch]]
            intra = [_mm(a_blk[ch][blk], v[:(blk + 1) * SUB]) for blk in range(n_sub)]
            inter = _mm_nt(q_b[bi][:, hcols[ch]] * eb_b[bi][:, hcols[ch]], st_old[ch])
            outs.append(jnp.concatenate(intra, axis=0) + inter)
        for ch, (bi, _) in enumerate(chains):
            b = b_b[bi][:, hcols[ch]]
            b_last = b[CHUNK - 1:CHUNK, :]
            v = i_ref[bi, rows, hcols[ch]]
            st_scr[ch] = (st_old[ch] * jnp.exp(b_last)
                          + _mm_tn(v, k_b[bi][:, hcols[ch]] * jnp.exp(b_last - b)))
            gate = gate_ref[bi, rows, hcols[ch]]
            o_ref[bi, rows, hcols[ch]] = (
                _rms(outs[ch], onorm) * _sigmoid(gate)).astype(o_ref.dtype)
        return carry

    lax.fori_loop(0, nc, chunk_body, 0)


def _hgrn(proj, lbp, onorm, bsz, t, tblk=256):
    nb = t // tblk
    blk = lambda col: pl.BlockSpec((bsz, tblk, D_MODEL), lambda i: (0, i, col))
    full = lambda a: pl.BlockSpec(a.shape, lambda i: (0,) * a.ndim)
    return pl.pallas_call(
        functools.partial(_hgrn_kernel, tblk=tblk, bsz=bsz),
        grid=(nb,),
        in_specs=[blk(4), blk(5), blk(6), blk(7), full(lbp), full(onorm)],
        out_specs=pl.BlockSpec((bsz, tblk, D_MODEL), lambda i: (0, i, 0)),
        out_shape=jax.ShapeDtypeStruct((bsz, t, D_MODEL), BF16),
        scratch_shapes=[pltpu.VMEM((bsz * HEADS, HEAD_DIM, HEAD_DIM), F32)],
        compiler_params=pltpu.CompilerParams(
            dimension_semantics=("arbitrary",), vmem_limit_bytes=VMEM_LIMIT),
        name="hgrn2",
    )(proj, proj, proj, proj, lbp, onorm)


def _cand_pairs():
    return [(a, b) for a in range(P_TOPK) for b in range(P_TOPK) if (a + 1) * (b + 1) <= P_TOPK]


def _extract_top(s, n, with_rank=False):
    tops = []
    cur = s
    rank = jnp.full(s.shape, RANK_NONE, F32) if with_rank else None
    for r in range(n):
        m = jnp.max(cur, axis=0, keepdims=True)
        tops.append(m)
        hit = cur == m
        if with_rank:
            rank = jnp.where(hit, float(r), rank)
        cur = jnp.where(hit, -jnp.inf, cur)
    return (tops, rank) if with_rank else tops


def _count_kept(s1, top2, tau):
    n = len(top2)
    assert n & (n - 1) == 0

    def row(masks, lo, step):
        if not masks:
            return top2[lo + step - 1]
        weight = step << len(masks)
        return jnp.where(masks[0], row(masks[1:], lo + weight, step), row(masks[1:], lo, step))

    masks, weights = [], []
    step = n // 2
    while step >= 1:
        masks.append(s1 + row(masks, 0, step) >= tau)
        weights.append(float(step))
        step //= 2
    masks.append(s1 + top2[n - 1] >= tau)
    weights.append(1.0)
    cnt = jnp.where(masks[0], weights[0], 0.0)
    for m, w in zip(masks[1:], weights[1:]):
        cnt = cnt + jnp.where(m, w, 0.0)
    return cnt


def _merge_kernel(oa_ref, ob_ref, ga_ref, gb_ref, x_ref, wa_ref, wb_ref, wo_ref, n2_ref,
                  wpq_ref, keys_ref, x1_ref, ht_ref, e1_ref, cnt_ref, e2_ref, rk_ref):
    ya = jnp.dot(oa_ref[...], wa_ref[...], preferred_element_type=F32)
    yb = jnp.dot(ob_ref[...], wb_ref[...], preferred_element_type=F32)
    mix = _sigmoid(ga_ref[...]) * ya + _sigmoid(gb_ref[...]) * yb
    x1 = x_ref[...] + _mm(mix, wo_ref[...])
    x1_ref[...] = x1
    h2 = _rms(x1, n2_ref[...])
    ht_ref[...] = h2.T.astype(BF16)
    qp = _mm(h2, wpq_ref[...])
    pairs = _cand_pairs()
    for h in range(P_HEADS):
        c0 = h * 2 * HEAD_DIM
        s1 = _mm_nt(keys_ref[0, h], qp[:, c0:c0 + HEAD_DIM])
        s2 = _mm_nt(keys_ref[1, h], qp[:, c0 + HEAD_DIM:c0 + 2 * HEAD_DIM])
        top1 = _extract_top(s1, P_TOPK)
        top2, rank2 = _extract_top(s2, P_TOPK, with_rank=True)
        cand = jnp.concatenate([top1[a] + top2[b] for a, b in pairs], axis=0)
        ctop = _extract_top(cand, P_TOPK)
        c_max = ctop[0]
        tau = ctop[P_TOPK - 1]
        z = jnp.exp(ctop[0] - c_max)
        for r_ in range(1, P_TOPK):
            z = z + jnp.exp(ctop[r_] - c_max)
        inv_z = 1.0 / z
        in1 = s1 >= top1[P_TOPK - 1]
        cnt = _count_kept(s1, top2, tau)
        e1_ref[h] = jnp.where(in1, jnp.exp(s1 - top1[0]), 0.0)
        cnt_ref[h] = jnp.where(in1, cnt, 0.0)
        e2_ref[h] = (jnp.where(rank2 < RANK_NONE, jnp.exp(s2 - top2[0]), 0.0) * inv_z).astype(BF16)
        rk_ref[h] = rank2.astype(BF16)


def _merge(oa, ob, proj, x2, wa, wb, wo, n2, wpq, keys, tb=256):
    nt = x2.shape[0]
    rowblk = lambda col: pl.BlockSpec((tb, D_MODEL), lambda i: (i, col))
    full = lambda a: pl.BlockSpec(a.shape, lambda i: (0,) * a.ndim)
    return pl.pallas_call(
        _merge_kernel,
        grid=(nt // tb,),
        in_specs=[rowblk(0), rowblk(0), rowblk(8), rowblk(9), rowblk(0),
                  full(wa), full(wb), full(wo), full(n2), full(wpq), full(keys)],
        out_specs=[
            pl.BlockSpec((tb, D_MODEL), lambda i: (i, 0)),
            pl.BlockSpec((D_MODEL, tb), lambda i: (0, i)),
            pl.BlockSpec((P_HEADS, N_KEYS, tb), lambda i: (0, 0, i)),
            pl.BlockSpec((P_HEADS, N_KEYS, tb), lambda i: (0, 0, i)),
            pl.BlockSpec((P_HEADS, N_KEYS, tb), lambda i: (0, 0, i)),
            pl.BlockSpec((P_HEADS, N_KEYS, tb), lambda i: (0, 0, i)),
        ],
        out_shape=[
            jax.ShapeDtypeStruct((nt, D_MODEL), F32),
            jax.ShapeDtypeStruct((D_MODEL, nt), BF16),
            jax.ShapeDtypeStruct((P_HEADS, N_KEYS, nt), F32),
            jax.ShapeDtypeStruct((P_HEADS, N_KEYS, nt), F32),
            jax.ShapeDtypeStruct((P_HEADS, N_KEYS, nt), BF16),
            jax.ShapeDtypeStruct((P_HEADS, N_KEYS, nt), BF16),
        ],
        compiler_params=pltpu.CompilerParams(
            dimension_semantics=("arbitrary",), vmem_limit_bytes=VMEM_LIMIT),
        name="merge",
    )(oa, ob, proj, proj, x2, wa, wb, wo, n2, wpq, keys)


def _peer_kernel(ht_ref, e1_ref, cnt_ref, e2_ref, rk_ref, u_ref, vt_ref, x1_ref, fn_ref, o_ref,
                 acc_scr, *, eb):
    j = pl.program_id(1)
    tb = ht_ref.shape[1]

    @pl.when(j == 0)
    def _():
        acc_scr[...] = jnp.zeros_like(acc_scr)

    ht = ht_ref[...]
    per = MXU_DEPTH // N_KEYS
    n_grp = eb // MXU_DEPTH
    grp_rows = [slice(g * MXU_DEPTH, (g + 1) * MXU_DEPTH) for g in range(n_grp)]
    hid_next = jnp.dot(u_ref[grp_rows[0], :], ht, preferred_element_type=F32)
    gated = []
    for g in range(n_grp):
        hid = hid_next
        if g + 1 < n_grp:
            hid_next = jnp.dot(u_ref[grp_rows[g + 1], :], ht, preferred_element_type=F32)
        parts = []
        for sub in range(per):
            i1 = j * (eb // N_KEYS) + g * per + sub
            wgt = None
            for h in range(P_HEADS):
                e1 = jnp.broadcast_to(e1_ref[h, pl.ds(i1, 1), :], (N_KEYS, tb)).astype(BF16)
                cnt = jnp.broadcast_to(cnt_ref[h, pl.ds(i1, 1), :], (N_KEYS, tb)).astype(BF16)
                term = jnp.where(rk_ref[h] < cnt, e2_ref[h] * e1, jnp.zeros((), BF16))
                wgt = term if wgt is None else wgt + term
            parts.append(wgt)
        wgt = jnp.concatenate(parts, axis=0)
        hb = hid.astype(BF16)
        half = hb * jnp.asarray(0.5, BF16)
        act = half * lax.erf(hb * jnp.asarray(2.0 ** -0.5, BF16)) + half
        gated.append(wgt * act)
    acc_scr[...] += jnp.dot(vt_ref[...], jnp.concatenate(gated, axis=0),
                            preferred_element_type=F32)

    @pl.when(j == pl.num_programs(1) - 1)
    def _():
        o_ref[...] = _rms(x1_ref[...] + acc_scr[...].T, fn_ref[...])


def _peer(ht, e1, cnt, e2, rk, u, vt, x1, fn, tb=512, eb=2048):
    nt = x1.shape[0]
    ne = u.shape[0]
    return pl.pallas_call(
        functools.partial(_peer_kernel, eb=eb),
        grid=(nt // tb, ne // eb),
        in_specs=[
            pl.BlockSpec((D_MODEL, tb), lambda i, j: (0, i)),
            pl.BlockSpec((P_HEADS, N_KEYS, tb), lambda i, j: (0, 0, i)),
            pl.BlockSpec((P_HEADS, N_KEYS, tb), lambda i, j: (0, 0, i)),
            pl.BlockSpec((P_HEADS, N_KEYS, tb), lambda i, j: (0, 0, i)),
            pl.BlockSpec((P_HEADS, N_KEYS, tb), lambda i, j: (0, 0, i)),
            pl.BlockSpec((eb, D_MODEL), lambda i, j: (j, 0)),
            pl.BlockSpec((D_MODEL, eb), lambda i, j: (0, j)),
            pl.BlockSpec((tb, D_MODEL), lambda i, j: (i, 0)),
            pl.BlockSpec((1, D_MODEL), lambda i, j: (0, 0)),
        ],
        out_specs=pl.BlockSpec((tb, D_MODEL), lambda i, j: (i, 0)),
        out_shape=jax.ShapeDtypeStruct((nt, D_MODEL), F32),
        scratch_shapes=[pltpu.VMEM((D_MODEL, tb), F32)],
        compiler_params=pltpu.CompilerParams(
            dimension_semantics=("arbitrary", "arbitrary"), vmem_limit_bytes=VMEM_LIMIT),
        name="peer",
    )(ht, e1, cnt, e2, rk, u, vt, x1, fn)


def _pad_lanes(row, offset):
    return jnp.zeros((1, 128), F32).at[0, offset:offset + row.shape[0]].set(row.astype(F32))


def kernel(x, norm1, w_in, conv_a, a_log, dt_bias, a_onorm, b_lower_bound, b_onorm, w_branch_a,
           w_branch_b, w_out, norm2, w_pq, sub_keys, expert_u, expert_v, final_norm):
    assert norm1.shape[0] == 1, "one layer"
    bsz, t, _ = x.shape
    nt = bsz * t
    x2 = x.reshape(nt, D_MODEL)

    wi = w_in[0]
    n_qkv = 3 * D_MODEL
    n_used = 10 * D_MODEL + 2 * HEADS
    w_main = jnp.concatenate([wi[:, :n_qkv], wi[:, n_qkv + 2 * HEADS:n_used]], axis=1).astype(BF16)
    w_small = jnp.pad(wi[:, n_qkv:n_qkv + 2 * HEADS], ((0, 0), (0, 128 - 2 * HEADS))).astype(BF16)
    w_small_t = wi[:, n_qkv:n_qkv + 2 * HEADS].T.astype(BF16)

    proj, small, small_t = _inproj(x2, norm1[0][None, :], w_main, w_small, w_small_t)

    alog_r = _pad_lanes(a_log[0], HEADS)
    dtb_r = _pad_lanes(dt_bias[0], HEADS)
    alog_c = alog_r[0, :16][:, None]
    dtb_c = dtb_r[0, :16][:, None]
    oa = _gdn(proj.reshape(bsz, t, -1), small.reshape(bsz, t, -1), small_t, conv_a[0], alog_r,
              dtb_r, alog_c, dtb_c, a_onorm[0][None, :], bsz, t).reshape(nt, D_MODEL)
    ob = _hgrn(proj.reshape(bsz, t, -1), b_lower_bound, b_onorm[0][None, :], bsz,
               t).reshape(nt, D_MODEL)

    x1, ht, e1, cnt, e2, rk = _merge(
        oa, ob, proj, x2, w_branch_a[0].astype(BF16), w_branch_b[0].astype(BF16),
        w_out[0].astype(BF16), norm2[0][None, :], w_pq[0].astype(BF16), sub_keys[0].astype(BF16))

    out = _peer(ht, e1, cnt, e2, rk, expert_u[0].astype(BF16), expert_v[0].T.astype(BF16), x1,
                final_norm[None, :])
    return out.reshape(bsz, t, D_MODEL)
```
